```python
import jax, jax.numpy as jnp
from jax import lax
import numpy as np

D_MODEL = 1024
BATCH = 8
SEQ = 4096
DEPTH = 4
DEC_BATCH = 2
DEC_SEQ = 16384
PAST_LEN = 128

N_MEM = 256
D_POOL = D_MODEL
POOL_WINDOWS = (2, 4, 8, 16)
N_POOL_GROUPS = len(POOL_WINDOWS)
POOL_GROUP = D_POOL // N_POOL_GROUPS
D_LRU = D_MODEL
N_LRU_BLOCKS = 8
LRU_BLOCK = D_LRU // N_LRU_BLOCKS
CONV_WIDTH = 4
CONV_LEFT = 2
LRU_C = 8.0
N_XHEADS = 4
D_XATTN = D_MODEL
XHEAD_DIM = D_XATTN // N_XHEADS
N_BRANCHES = 3
D_IN = D_POOL + D_LRU + D_XATTN + N_BRANCHES * D_MODEL
D_FF = 4 * D_MODEL
EPS = 1e-6

kernel_name = "hybrid_pool_rglru_memxattn_encoder"


def _rms_norm(x, gain):
    xf = x.astype(jnp.float32)
    y = xf * lax.rsqrt(jnp.mean(xf * xf, axis=-1, keepdims=True) + EPS)
    return (y * gain.astype(jnp.float32)).astype(x.dtype)


def _pool_mixer(u, pool_w, pool_scale):
    B, S, _ = u.shape
    uf = u.astype(jnp.float32)
    cs = jnp.pad(jnp.cumsum(uf, axis=1), ((0, 0), (1, 0), (0, 0)))
    t = jnp.arange(S)
    outs = []
    for g, w in enumerate(POOL_WINDOWS):
        h = w // 2
        lo_c, hi_c = g * POOL_GROUP, (g + 1) * POOL_GROUP
        csg = jnp.pad(cs[..., lo_c:hi_c], ((0, 0), (h, h), (0, 0)), mode="edge")
        win_sum = csg[:, 2 * h:2 * h + S] - csg[:, :S]
        count = (jnp.minimum(t + h, S) - jnp.maximum(t - h, 0)).astype(jnp.float32)
        pooled = win_sum / count[None, :, None] - uf[..., lo_c:hi_c]
        outs.append(jnp.einsum("bsc,cd->bsd", pooled.astype(u.dtype), pool_w[g]))
    return jnp.concatenate(outs, axis=-1) * pool_scale


def _conv_centred(u, conv_w, conv_b):
    S = u.shape[1]
    up = jnp.pad(u, ((0, 0), (CONV_LEFT, CONV_WIDTH - 1 - CONV_LEFT), (0, 0)))
    y = conv_b + up[:, 0:S] * conv_w[0]
    for k in range(1, CONV_WIDTH):
        y = y + up[:, k:k + S] * conv_w[k]
    return y


def _lin_combine(e1, e2):
    a1, b1 = e1
    a2, b2 = e2
    return (a1 * a2, a2 * b1 + b2)


def _rglru_direction(xc, wa, ba, wx, bx, lam, reverse):
    B, S, C = xc.shape
    xb = xc.reshape(B, S, N_LRU_BLOCKS, LRU_BLOCK)
    r = jax.nn.sigmoid((jnp.einsum("bshi,hij->bshj", xb, wa).reshape(B, S, C) + ba).astype(jnp.float32))
    i = jax.nn.sigmoid((jnp.einsum("bshi,hij->bshj", xb, wx).reshape(B, S, C) + bx).astype(jnp.float32))
    log_a = -LRU_C * r * jax.nn.softplus(-lam.astype(jnp.float32))
    a = jnp.exp(log_a)
    mult = jnp.sqrt(-jnp.expm1(2.0 * log_a))
    b = mult * i * xc.astype(jnp.float32)
    _, h = lax.associative_scan(_lin_combine, (a, b), reverse=reverse, axis=1)
    return h


def _memory_attention(q, mem_n, w_kv):
    B, S, _ = q.shape
    M = mem_n.shape[1]
    kv = jnp.einsum("bmd,de->bme", mem_n, w_kv)
    k = kv[..., :D_XATTN].reshape(B, M, N_XHEADS, XHEAD_DIM)
    v = kv[..., D_XATTN:].reshape(B, M, N_XHEADS, XHEAD_DIM)
    qh = q.reshape(B, S, N_XHEADS, XHEAD_DIM)
    s = jnp.einsum("bshd,bmhd->bhsm", qh, k).astype(jnp.float32) * (XHEAD_DIM ** -0.5)
    p = jax.nn.softmax(s, axis=-1).astype(v.dtype)
    return jnp.einsum("bhsm,bmhd->bshd", p, v).reshape(B, S, D_XATTN)


def _mixer_sublayer(x, mem, g_pre, g_post, g_mem, w_in, pool_w, pool_scale, conv_w, conv_b,
                    lru_wa, lru_ba, lru_wx, lru_bx, lru_lambda, w_kv, w_out):
    B, S, _ = x.shape
    h = _rms_norm(x, g_pre)
    z = jnp.einsum("bsd,de->bse", h, w_in)
    o1, o2, o3 = D_POOL, D_POOL + D_LRU, D_POOL + D_LRU + D_XATTN
    u_pool, u_lru, q, gate_logits = z[..., :o1], z[..., o1:o2], z[..., o2:o3], z[..., o3:]
    y_pool = _pool_mixer(u_pool, pool_w, pool_scale).astype(jnp.float32)
    xc = _conv_centred(u_lru, conv_w, conv_b)
    y_lru = (_rglru_direction(xc, lru_wa[0], lru_ba[0], lru_wx[0], lru_bx[0], lru_lambda[0], False)
             + _rglru_direction(xc, lru_wa[1], lru_ba[1], lru_wx[1], lru_bx[1], lru_lambda[1], True))
    y_mem = _memory_attention(q, _rms_norm(mem, g_mem), w_kv).astype(jnp.float32)
    gates = jax.nn.sigmoid(gate_logits.astype(jnp.float32)).reshape(B, S, N_BRANCHES, D_MODEL)
    merged = gates[..., 0, :] * y_pool + gates[..., 1, :] * y_lru + gates[..., 2, :] * y_mem
    out = jnp.einsum("bsd,de->bse", merged.astype(x.dtype), w_out)
    return x + _rms_norm(out, g_post)


def _mlp_sublayer(x, g_pre, g_post, w1, w2):
    h = _rms_norm(x, g_pre)
    a = jax.nn.relu(jnp.einsum("bsd,df->bsf", h, w1))
    y = jnp.einsum("bsf,fd->bsd", a * a, w2)
    return x + _rms_norm(y, g_post)


def _trunk(x, mem, norm_mix_pre, norm_mix_post, norm_mem, w_in, pool_w, pool_scale, conv_w, conv_b,
           lru_wa, lru_ba, lru_wx, lru_bx, lru_lambda, w_kv, w_out,
           norm_mlp_pre, norm_mlp_post, mlp_w1, mlp_w2):
    for l in range(DEPTH):
        x = _mixer_sublayer(x, mem, norm_mix_pre[l], norm_mix_post[l], norm_mem[l], w_in[l],
                            pool_w[l], pool_scale[l], conv_w[l], conv_b[l],
                            lru_wa[l], lru_ba[l], lru_wx[l], lru_bx[l], lru_lambda[l],
                            w_kv[l], w_out[l])
        x = _mlp_sublayer(x, norm_mlp_pre[l], norm_mlp_post[l], mlp_w1[l], mlp_w2[l])
    return x


def setup_inputs(seed: int = 0) -> dict:
    key = jax.random.key(seed)
    ks = jax.random.split(key, 24)
    f32 = jnp.float32

    def nrm(k, shape, scale):
        return jax.random.normal(k, shape, f32) * scale

    def gain(k):
        return 1.0 + 0.05 * jax.random.normal(k, (DEPTH, D_MODEL), f32)

    a0 = jax.random.uniform(ks[13], (DEPTH, 2, D_LRU), f32, minval=0.9, maxval=0.999)
    return {
        "x_prompt": jax.random.normal(ks[0], (BATCH, SEQ, D_MODEL), f32),
        "x_sample": jax.random.normal(ks[1], (DEC_BATCH, DEC_SEQ, D_MODEL), f32),
        "mem_prompt": jax.random.normal(ks[2], (BATCH, N_MEM, D_MODEL), f32),
        "mem_sample": jax.random.normal(ks[3], (DEC_BATCH, N_MEM, D_MODEL), f32),
        "norm_mix_pre": gain(ks[4]),
        "norm_mix_post": gain(ks[5]),
        "norm_mem": gain(ks[6]),
        "w_in": nrm(ks[7], (DEPTH, D_MODEL, D_IN), D_MODEL ** -0.5),
        "pool_w": nrm(ks[8], (DEPTH, N_POOL_GROUPS, POOL_GROUP, POOL_GROUP), POOL_GROUP ** -0.5),
        "pool_scale": 1.0 + 0.1 * jax.random.normal(ks[9], (DEPTH, D_POOL), f32),
        "conv_w": nrm(ks[10], (DEPTH, CONV_WIDTH, D_LRU), CONV_WIDTH ** -0.5),
        "conv_b": nrm(ks[11], (DEPTH, D_LRU), 0.02),
        "lru_wa": nrm(ks[12], (DEPTH, 2, N_LRU_BLOCKS, LRU_BLOCK, LRU_BLOCK), LRU_BLOCK ** -0.5),
        "lru_ba": nrm(ks[14], (DEPTH, 2, D_LRU), 0.02),
        "lru_wx": nrm(ks[15], (DEPTH, 2, N_LRU_BLOCKS, LRU_BLOCK, LRU_BLOCK), LRU_BLOCK ** -0.5),
        "lru_bx": nrm(ks[16], (DEPTH, 2, D_LRU), 0.02),
        "lru_lambda": jnp.log(a0) - jnp.log1p(-a0),
        "w_kv": nrm(ks[17], (DEPTH, D_MODEL, 2 * D_XATTN), D_MODEL ** -0.5),
        "w_out": nrm(ks[18], (DEPTH, D_MODEL, D_MODEL), D_MODEL ** -0.5),
        "norm_mlp_pre": gain(ks[19]),
        "norm_mlp_post": gain(ks[20]),
        "mlp_w1": nrm(ks[21], (DEPTH, D_MODEL, D_FF), D_MODEL ** -0.5),
        "mlp_w2": nrm(ks[22], (DEPTH, D_FF, D_MODEL), D_FF ** -0.5),
    }


def reference(x_prompt, x_sample, mem_prompt, mem_sample, norm_mix_pre, norm_mix_post, norm_mem,
              w_in, pool_w, pool_scale, conv_w, conv_b, lru_wa, lru_ba, lru_wx, lru_bx, lru_lambda,
              w_kv, w_out, norm_mlp_pre, norm_mlp_post, mlp_w1, mlp_w2):
    y_prompt = _trunk(x_prompt, mem_prompt, norm_mix_pre, norm_mix_post, norm_mem, w_in, pool_w,
                      pool_scale, conv_w, conv_b, lru_wa, lru_ba, lru_wx, lru_bx, lru_lambda,
                      w_kv, w_out, norm_mlp_pre, norm_mlp_post, mlp_w1, mlp_w2)
    y_sample = _trunk(x_sample, mem_sample, norm_mix_pre, norm_mix_post, norm_mem, w_in, pool_w,
                      pool_scale, conv_w, conv_b, lru_wa, lru_ba, lru_wx, lru_bx, lru_lambda,
                      w_kv, w_out, norm_mlp_pre, norm_mlp_post, mlp_w1, mlp_w2)
    return (y_prompt, y_sample)
```

```python
import functools

import jax
import jax.numpy as jnp
from jax import lax
from jax.experimental import pallas as pl
from jax.experimental.pallas import tpu as pltpu

F32 = jnp.float32
BF16 = jnp.bfloat16

LANES = 128
SUBLANES = 8
EPS = 1e-6
LRU_C = 8.0
POOL_WINDOWS = (2, 4, 8, 16)
N_POOL_GROUPS = len(POOL_WINDOWS)
N_XHEADS = 4
CONV_WIDTH = 4
CONV_LEFT = 2
N_LRU_BLOCKS = 8
HALO = 8
PAD_ROWS = 32
VMEM_LIMIT = 56 * 1024 * 1024

TOKEN_TILE = 512
TIME_CHUNK = 256


def _rms(x, gain):
    ms = jnp.mean(x * x, axis=-1, keepdims=True)
    return x * lax.rsqrt(ms + EPS) * gain


def _const_spec(shape):
    nd = len(shape)
    return pl.BlockSpec(shape, lambda *_: (0,) * nd)


def _inproj_body(x_ref, g_ref, w_ref, up_ref, ul_ref, q_ref, gl_ref):
    d = x_ref.shape[1]
    h = _rms(x_ref[...], g_ref[...]).astype(BF16)
    up_ref[...] = jnp.dot(h, w_ref[:, 0:d], preferred_element_type=F32)
    ul_ref[...] = jnp.dot(h, w_ref[:, d:2 * d], preferred_element_type=F32)
    q_ref[...] = jnp.dot(h, w_ref[:, 2 * d:3 * d], preferred_element_type=F32).astype(BF16)
    gl_ref[...] = jnp.dot(h, w_ref[:, 3 * d:6 * d], preferred_element_type=F32)


def _inproj(x2, gain, w_in):
    t, d = x2.shape
    tm = min(TOKEN_TILE, t)
    row = lambda i: (i, 0)
    return pl.pallas_call(
        _inproj_body,
        grid=(t // tm,),
        in_specs=[pl.BlockSpec((tm, d), row), _const_spec((1, d)),
                  pl.BlockSpec(w_in.shape, lambda i: (0, 0), pipeline_mode=pl.Buffered(1))],
        out_specs=[pl.BlockSpec((tm, d), row), pl.BlockSpec((tm, d), row),
                   pl.BlockSpec((tm, d), row), pl.BlockSpec((tm, 3 * d), row)],
        out_shape=[jax.ShapeDtypeStruct((t, d), F32), jax.ShapeDtypeStruct((t, d), F32),
                   jax.ShapeDtypeStruct((t, d), BF16), jax.ShapeDtypeStruct((t, 3 * d), F32)],
        compiler_params=pltpu.CompilerParams(dimension_semantics=("arbitrary",),
                                             vmem_limit_bytes=VMEM_LIMIT),
        name="inproj",
    )(x2, gain, w_in)


def _memkv_body(scale, mem_ref, g_ref, wk_ref, wv_ref, kt_ref, v_ref):
    mn = _rms(mem_ref[...], g_ref[...]).astype(BF16)
    k = jnp.dot(mn, wk_ref[...], preferred_element_type=F32)
    kt_ref[...] = (k.T * scale).astype(BF16)
    v_ref[...] = jnp.dot(mn, wv_ref[...], preferred_element_type=F32).astype(BF16)


def _memkv(mem, gain, wk, wv):
    b, m, d = mem.shape
    scale = float((d // N_XHEADS) ** -0.5)
    return pl.pallas_call(
        functools.partial(_memkv_body, scale),
        grid=(b,),
        in_specs=[pl.BlockSpec((None, m, d), lambda i: (i, 0, 0)), _const_spec((1, d)),
                  _const_spec(wk.shape), _const_spec(wv.shape)],
        out_specs=[pl.BlockSpec((None, d, m), lambda i: (i, 0, 0)),
                   pl.BlockSpec((None, m, d), lambda i: (i, 0, 0))],
        out_shape=[jax.ShapeDtypeStruct((b, d, m), BF16), jax.ShapeDtypeStruct((b, m, d), BF16)],
        compiler_params=pltpu.CompilerParams(dimension_semantics=("arbitrary",),
                                             vmem_limit_bytes=VMEM_LIMIT),
        name="memkv",
    )(mem, gain, wk, wv)


def _shifted(buf, col, start, rows):
    return buf[pl.ds(col, 1, stride=2), pl.ds(start, rows), :].reshape(rows, LANES)


def _fill_halo_buf(buf, cur_ref, prev_ref, next_ref, has_prev, has_next):
    tc = cur_ref.shape[0]
    ncol = cur_ref.shape[1] // LANES
    for c in range(ncol):
        cs = pl.ds(c * LANES, LANES)
        buf[c, pl.ds(0, HALO), :] = jnp.where(has_prev, prev_ref[:, cs], 0.0)
        buf[c, pl.ds(HALO, tc), :] = cur_ref[:, cs]
        buf[c, pl.ds(HALO + tc, HALO), :] = jnp.where(has_next, next_ref[:, cs], 0.0)
        buf[c, pl.ds(2 * HALO + tc, PAD_ROWS - 2 * HALO), :] = jnp.zeros((PAD_ROWS - 2 * HALO, LANES), F32)


def _lru_direction(reverse, first_step, ubuf, convw_ref, convb_ref, wg_ref, ba_ref, bx_ref, lam_ref,
                   xq, qa, qb, qh, carry_ref):
    ncol, tc = xq.shape[0], xq.shape[1]
    sub = tc // SUBLANES

    @pl.when(first_step)
    def _():
        carry_ref[...] = jnp.zeros(carry_ref.shape, F32)

    for c in range(ncol):
        cs = pl.ds(c * LANES, LANES)
        w = convw_ref[:, cs]
        xc = convb_ref[:, cs] + _shifted(ubuf, c, HALO - CONV_LEFT, tc) * w[0:1]
        for k in range(1, CONV_WIDTH):
            xc = xc + _shifted(ubuf, c, HALO - CONV_LEFT + k, tc) * w[k:k + 1]
        for j in range(SUBLANES):
            xq[c, pl.ds(j, sub, stride=SUBLANES), :] = xc[j * sub:(j + 1) * sub]

    for c in range(ncol):
        cs = pl.ds(c * LANES, LANES)
        xb = xq[c]
        g = jnp.dot(xb.astype(BF16), wg_ref[c], preferred_element_type=F32)
        tr = jnp.tanh(g[:, :LANES] + ba_ref[:, cs])
        ti = jnp.tanh(g[:, LANES:] + bx_ref[:, cs])
        nk = (-0.5 * LRU_C) * jax.nn.softplus(-lam_ref[:, cs])
        log_a = nk * tr + nk
        a = jnp.exp(log_a)
        m2 = jnp.tanh(log_a) * (-1.0 - a * a)
        mult = m2 * lax.rsqrt(jnp.maximum(m2, 1e-30))
        qa[c] = a
        qb[c] = mult * (0.5 * ti + 0.5) * xb

    def step(s):
        k = (sub - 1 - s) if reverse else s
        return pl.ds(pl.multiple_of(k * SUBLANES, SUBLANES), SUBLANES)

    def pass1(s, hp):
        h, p = hp
        rows = step(s)
        a = qa[:, rows, :]
        return a * h + qb[:, rows, :], a * p

    shape = (ncol, SUBLANES, LANES)
    f, pf = lax.fori_loop(0, sub, pass1, (jnp.zeros(shape, F32), jnp.ones(shape, F32)))

    row = lax.broadcasted_iota(jnp.int32, (SUBLANES, LANES), 0)
    edge = (SUBLANES - 1) if reverse else 0
    shift = (SUBLANES - 1) if reverse else 1
    inits = []
    for c in range(ncol):
        carry = carry_ref[c]
        init = carry
        for _ in range(SUBLANES - 1):
            init = jnp.where(row == edge, carry, pltpu.roll(f[c] + pf[c] * init, shift, 0))
        last = f[c] + pf[c] * init
        out_row = 0 if reverse else SUBLANES - 1
        carry_ref[c] = jnp.broadcast_to(last[out_row:out_row + 1, :], (SUBLANES, LANES))
        inits.append(init)
    h0 = jnp.stack(inits, axis=0)

    def pass2(s, h):
        rows = step(s)
        h = qa[:, rows, :] * h + qb[:, rows, :]
        qh[:, rows, :] = h
        return h

    lax.fori_loop(0, sub, pass2, h0)


def _unpermute(qh, col, tc):
    sub = tc // SUBLANES
    return jnp.concatenate(
        [qh[col, pl.ds(j, sub, stride=SUBLANES), :] for j in range(SUBLANES)], axis=0)


def _halo_specs(tc, d, s_len, chunk_of):
    per = tc // HALO
    last = s_len // HALO - 1
    cur = pl.BlockSpec((None, tc, d), lambda b, i: (b, chunk_of(i), 0))
    prev = pl.BlockSpec((None, HALO, d), lambda b, i: (b, jnp.maximum(chunk_of(i) * per - 1, 0), 0))
    nxt = pl.BlockSpec((None, HALO, d), lambda b, i: (b, jnp.minimum((chunk_of(i) + 1) * per, last), 0))
    return [cur, prev, nxt]


def _lru_bwd_body(ul_ref, ulp_ref, uln_ref, convw_ref, convb_ref, wg_ref, ba_ref, bx_ref, lam_ref,
                  hb_ref, ubuf, xq, qa, qb, qh, carry_ref):
    i = pl.program_id(1)
    nc = pl.num_programs(1)
    ti = nc - 1 - i
    tc = ul_ref.shape[0]
    _fill_halo_buf(ubuf, ul_ref, ulp_ref, uln_ref, ti > 0, ti < nc - 1)
    _lru_direction(True, i == 0, ubuf, convw_ref, convb_ref, wg_ref, ba_ref, bx_ref, lam_ref,
                   xq, qa, qb, qh, carry_ref)
    for c in range(xq.shape[0]):
        hb_ref[:, pl.ds(c * LANES, LANES)] = _unpermute(qh, c, tc)


def _lru_scratch(tc, d):
    ncol = d // LANES
    return [pltpu.VMEM((ncol, tc + PAD_ROWS, LANES), F32)] + \
           [pltpu.VMEM((ncol, tc, LANES), F32) for _ in range(4)] + \
           [pltpu.VMEM((ncol, SUBLANES, LANES), F32)]


def _lru_bwd(ul, convw, convb, wg, ba, bx, lam):
    b, s_len, d = ul.shape
    tc = min(TIME_CHUNK, s_len)
    nc = s_len // tc
    rev = lambda i: nc - 1 - i
    return pl.pallas_call(
        _lru_bwd_body,
        grid=(b, nc),
        in_specs=_halo_specs(tc, d, s_len, rev) + [
            _const_spec(convw.shape), _const_spec(convb.shape), _const_spec(wg.shape),
            _const_spec(ba.shape), _const_spec(bx.shape), _const_spec(lam.shape)],
        out_specs=pl.BlockSpec((None, tc, d), lambda bb, i: (bb, rev(i), 0)),
        out_shape=jax.ShapeDtypeStruct((b, s_len, d), F32),
        scratch_shapes=_lru_scratch(tc, d),
        compiler_params=pltpu.CompilerParams(dimension_semantics=("arbitrary", "arbitrary"),
                                             vmem_limit_bytes=VMEM_LIMIT),
        name="lru_bwd",
    )(ul, ul, ul, convw, convb, wg, ba, bx, lam)


def _window_sums(pbuf, t1, t2, group, tc):
    outs = []
    for c in (2 * group, 2 * group + 1):
        if group == 0:
            outs.append(_shifted(pbuf, c, 7, tc) + pbuf[c, pl.ds(8, tc), :])
        elif group == 1:
            outs.append((_shifted(pbuf, c, 6, tc) + _shifted(pbuf, c, 7, tc))
                        + (pbuf[c, pl.ds(8, tc), :] + _shifted(pbuf, c, 9, tc)))
        else:
            n2 = tc + 24 if group == 3 else tc + 16
            t1[0, pl.ds(0, n2), :] = pbuf[c, pl.ds(0, n2), :] + _shifted(pbuf, c, 1, n2)
            n4 = n2 - 8
            t2[0, pl.ds(0, n4), :] = t1[0, pl.ds(0, n4), :] + _shifted(t1, 0, 2, n4)
            if group == 2:
                outs.append(_shifted(t2, 0, 4, tc) + t2[0, pl.ds(8, tc), :])
            else:
                n8 = n4 - 8
                t1[0, pl.ds(0, n8), :] = t2[0, pl.ds(0, n8), :] + _shifted(t2, 0, 4, n8)
                outs.append(t1[0, pl.ds(0, tc), :] + t1[0, pl.ds(8, tc), :])
    return outs


def _mixer_body(s_len, x_ref, up_ref, upp_ref, upn_ref, ul_ref, ulp_ref, uln_ref, q_ref, gl_ref, hb_ref,
                kt_ref, v_ref, poolw_ref, pscale_ref, convw_ref, convb_ref, wg_ref, ba_ref, bx_ref,
                lam_ref, wout_ref, gpost_ref, o_ref,
                pbuf, ubuf, xq, qa, qb, qh, carry_ref, t1, t2, mbuf):
    i = pl.program_id(1)
    nc = pl.num_programs(1)
    tc, d = x_ref.shape
    has_prev, has_next = i > 0, i < nc - 1
    _fill_halo_buf(pbuf, up_ref, upp_ref, upn_ref, has_prev, has_next)
    _fill_halo_buf(ubuf, ul_ref, ulp_ref, uln_ref, has_prev, has_next)
    _lru_direction(False, i == 0, ubuf, convw_ref, convb_ref, wg_ref, ba_ref, bx_ref, lam_ref,
                   xq, qa, qb, qh, carry_ref)

    t_glob = i * tc + lax.broadcasted_iota(jnp.int32, (tc, LANES), 0)
    gw = d // N_POOL_GROUPS
    for g in range(N_POOL_GROUPS):
        cols = pl.ds(g * gw, gw)
        half = POOL_WINDOWS[g] // 2
        count = jnp.minimum(t_glob + half, s_len) - jnp.maximum(t_glob - half, 0)
        inv = 1.0 / count.astype(F32)
        sums = _window_sums(pbuf, t1, t2, g, tc)
        pooled = jnp.concatenate(
            [sums[k] * inv - pbuf[2 * g + k, pl.ds(HALO, tc), :] for k in range(2)], axis=1)
        yp = jnp.dot(pooled.astype(BF16), poolw_ref[g], preferred_element_type=F32) * pscale_ref[:, cols]

        sc = jnp.dot(q_ref[:, cols], kt_ref[cols, :], preferred_element_type=F32)
        e = jnp.exp(sc - jnp.max(sc, axis=-1, keepdims=True))
        p = e * (1.0 / jnp.sum(e, axis=-1, keepdims=True))
        ym = jnp.dot(p.astype(BF16), v_ref[:, cols], preferred_element_type=F32)

        yl = jnp.concatenate([_unpermute(qh, 2 * g, tc), _unpermute(qh, 2 * g + 1, tc)], axis=1) + hb_ref[:, cols]

        merged = (yp + yl + ym) + jnp.tanh(gl_ref[:, cols]) * yp \
            + jnp.tanh(gl_ref[:, pl.ds(d + g * gw, gw)]) * yl \
            + jnp.tanh(gl_ref[:, pl.ds(2 * d + g * gw, gw)]) * ym
        mbuf[:, cols] = merged.astype(BF16)

    out = jnp.dot(mbuf[...], wout_ref[...], preferred_element_type=F32)
    o_ref[...] = x_ref[...] + _rms(out, gpost_ref[...])


def _mixer(x, up, ul, q, gl, hb, kt, v, poolw, pscale, convw, convb, wg, ba, bx, lam, wout, gpost):
    b, s_len, d = x.shape
    tc = min(TIME_CHUNK, s_len)
    nc = s_len // tc
    ncol = d // LANES
    fwd = lambda i: i
    cur = lambda w: pl.BlockSpec((None, tc, w), lambda bb, i: (bb, i, 0))
    per_batch = lambda a: pl.BlockSpec((None,) + a.shape[1:], lambda bb, i: (bb, 0, 0))
    consts = [poolw, pscale, convw, convb, wg, ba, bx, lam, wout, gpost]
    scratch = [pltpu.VMEM((ncol, tc + PAD_ROWS, LANES), F32)] + _lru_scratch(tc, d) + [
        pltpu.VMEM((1, tc + PAD_ROWS, LANES), F32), pltpu.VMEM((1, tc + PAD_ROWS, LANES), F32),
        pltpu.VMEM((tc, d), BF16)]
    return pl.pallas_call(
        functools.partial(_mixer_body, s_len),
        grid=(b, nc),
        in_specs=[cur(d)] + _halo_specs(tc, d, s_len, fwd) + _halo_specs(tc, d, s_len, fwd)
                 + [cur(d), cur(3 * d), cur(d), per_batch(kt), per_batch(v)]
                 + [_const_spec(a.shape) for a in consts],
        out_specs=cur(d),
        out_shape=jax.ShapeDtypeStruct((b, s_len, d), F32),
        scratch_shapes=scratch,
        compiler_params=pltpu.CompilerParams(dimension_semantics=("arbitrary", "arbitrary"),
                                             vmem_limit_bytes=VMEM_LIMIT),
        name="mixer",
    )(x, up, up, up, ul, ul, ul, q, gl, hb, kt, v, *consts)


def _mlp_body(x_ref, gpre_ref, gpost_ref, w1_ref, w2_ref, o_ref):
    d = x_ref.shape[1]
    x = x_ref[...]
    h = _rms(x, gpre_ref[...]).astype(BF16)
    y = None
    for f in range(w1_ref.shape[1] // d):
        a = jnp.maximum(jnp.dot(h, w1_ref[:, f * d:(f + 1) * d], preferred_element_type=F32), 0.0)
        part = jnp.dot((a * a).astype(BF16), w2_ref[f * d:(f + 1) * d, :], preferred_element_type=F32)
        y = part if y is None else y + part
    o_ref[...] = x + _rms(y, gpost_ref[...])


def _mlp(x2, gpre, gpost, w1, w2):
    t, d = x2.shape
    tm = min(TOKEN_TILE, t)
    row = lambda i: (i, 0)
    return pl.pallas_call(
        _mlp_body,
        grid=(t // tm,),
        in_specs=[pl.BlockSpec((tm, d), row), _const_spec((1, d)), _const_spec((1, d)),
                  pl.BlockSpec(w1.shape, lambda i: (0, 0), pipeline_mode=pl.Buffered(1)),
                  pl.BlockSpec(w2.shape, lambda i: (0, 0), pipeline_mode=pl.Buffered(1))],
        out_specs=pl.BlockSpec((tm, d), row),
        out_shape=jax.ShapeDtypeStruct((t, d), F32),
        compiler_params=pltpu.CompilerParams(dimension_semantics=("arbitrary",),
                                             vmem_limit_bytes=VMEM_LIMIT),
        name="mlp",
    )(x2, gpre, gpost, w1, w2)


def _prep_layer(l, p):
    d = p["w_in"].shape[1]
    row = lambda a: a.reshape(1, -1)
    w_in = p["w_in"][l]
    w_in = jnp.concatenate([w_in[:, :3 * d], 0.5 * w_in[:, 3 * d:]], axis=1).astype(BF16)
    dirs = []
    for k in range(2):
        wg = (0.5 * jnp.concatenate([p["lru_wa"][l, k], p["lru_wx"][l, k]], axis=-1)).astype(BF16)
        dirs.append((wg, row(0.5 * p["lru_ba"][l, k]), row(0.5 * p["lru_bx"][l, k]), row(p["lru_lambda"][l, k])))
    return dict(
        g_pre=row(p["norm_mix_pre"][l]), g_post=row(p["norm_mix_post"][l]), g_mem=row(p["norm_mem"][l]),
        w_in=w_in, pool_w=p["pool_w"][l].astype(BF16), pool_scale=row(p["pool_scale"][l]),
        conv_w=p["conv_w"][l], conv_b=row(p["conv_b"][l]), lru=dirs,
        wk=p["w_kv"][l][:, :d].astype(BF16), wv=p["w_kv"][l][:, d:].astype(BF16),
        w_out=(0.5 * p["w_out"][l]).astype(BF16),
        m_pre=row(p["norm_mlp_pre"][l]), m_post=row(p["norm_mlp_post"][l]),
        w1=p["mlp_w1"][l].astype(BF16), w2=p["mlp_w2"][l].astype(BF16),
    )


def _trunk(x, mem, layers):
    b, s_len, d = x.shape
    for lp in layers:
        up, ul, q, gl = _inproj(x.reshape(b * s_len, d), lp["g_pre"], lp["w_in"])
        up, ul, q = (a.reshape(b, s_len, d) for a in (up, ul, q))
        gl = gl.reshape(b, s_len, 3 * d)
        kt, v = _memkv(mem, lp["g_mem"], lp["wk"], lp["wv"])
        hb = _lru_bwd(ul, lp["conv_w"], lp["conv_b"], *lp["lru"][1])
        x = _mixer(x, up, ul, q, gl, hb, kt, v, lp["pool_w"], lp["pool_scale"], lp["conv_w"], lp["conv_b"],
                   *lp["lru"][0], lp["w_out"], lp["g_post"])
        x = _mlp(x.reshape(b * s_len, d), lp["m_pre"], lp["m_post"], lp["w1"], lp["w2"]).reshape(b, s_len, d)
    return x


def kernel(x_prompt, x_sample, mem_prompt, mem_sample, norm_mix_pre, norm_mix_post, norm_mem, w_in, pool_w, pool_scale, conv_w, conv_b, lru_wa, lru_ba, lru_wx, lru_bx, lru_lambda, w_kv, w_out, norm_mlp_pre, norm_mlp_post, mlp_w1, mlp_w2):
    p = dict(norm_mix_pre=norm_mix_pre, norm_mix_post=norm_mix_post, norm_mem=norm_mem, w_in=w_in,
             pool_w=pool_w, pool_scale=pool_scale, conv_w=conv_w, conv_b=conv_b, lru_wa=lru_wa,
             lru_ba=lru_ba, lru_wx=lru_wx, lru_bx=lru_bx, lru_lambda=lru_lambda, w_kv=w_kv, w_out=w_out,
             norm_mlp_pre=norm_mlp_pre, norm_mlp_post=norm_mlp_post, mlp_w1=mlp_w1, mlp_w2=mlp_w2)
    layers = [_prep_layer(l, p) for l in range(w_in.shape[0])]
    return (_trunk(x_prompt, mem_prompt, layers), _trunk(x_sample, mem_sample, layers))
```

```python
import functools

import jax
import jax.numpy as jnp
from jax import lax
from jax.experimental import pallas as pl
from jax.experimental.pallas import tpu as pltpu

F32 = jnp.float32
BF16 = jnp.bfloat16

LANES = 128
SUBLANES = 8
EPS = 1e-6
LRU_C = 8.0
POOL_WINDOWS = (2, 4, 8, 16)
N_POOL_GROUPS = len(POOL_WINDOWS)
N_XHEADS = 4
CONV_WIDTH = 4
CONV_LEFT = 2
HALO = 8
PAD_ROWS = 32
VMEM_LIMIT = 56 * 1024 * 1024

TOKEN_TILE = 512
TIME_CHUNK = 256


def _rms(x, gain):
    ms = jnp.mean(x * x, axis=-1, keepdims=True)
    return x * lax.rsqrt(ms + EPS) * gain


def _const_spec(shape):
    nd = len(shape)
    return pl.BlockSpec(shape, lambda *_: (0,) * nd)


def _resident_spec(shape):
    nd = len(shape)
    return pl.BlockSpec(shape, lambda *_: (0,) * nd, pipeline_mode=pl.Buffered(1))


def _shifted(buf, col, start, rows):
    return buf[pl.ds(col, 1, stride=2), pl.ds(start, rows), :].reshape(rows, LANES)


def _halo_specs(tc, d, s_len, chunk_of):
    per = tc // HALO
    last = s_len // HALO - 1
    cur = pl.BlockSpec((None, tc, d), lambda b, i: (b, chunk_of(i), 0))
    prev = pl.BlockSpec((None, HALO, d), lambda b, i: (b, jnp.maximum(chunk_of(i) * per - 1, 0), 0))
    nxt = pl.BlockSpec((None, HALO, d), lambda b, i: (b, jnp.minimum((chunk_of(i) + 1) * per, last), 0))
    return [cur, prev, nxt]


def _scan(reverse, first_step, a_ref, b_ref, carry_ref, emit):
    tc, d = a_ref.shape
    sub = tc // SUBLANES

    @pl.when(first_step)
    def _():
        carry_ref[...] = jnp.zeros(carry_ref.shape, F32)

    def step(s):
        k = (sub - 1 - s) if reverse else s
        return pl.ds(pl.multiple_of(k * SUBLANES, SUBLANES), SUBLANES)

    def pass1(s, hp):
        h, p = hp
        rows = step(s)
        a = a_ref[rows, :]
        return a * h + b_ref[rows, :], a * p

    shape = (SUBLANES, d)
    f, pf = lax.fori_loop(0, sub, pass1, (jnp.zeros(shape, F32), jnp.ones(shape, F32)), unroll=True)

    row = lax.broadcasted_iota(jnp.int32, shape, 0)
    edge = (SUBLANES - 1) if reverse else 0
    shift = (SUBLANES - 1) if reverse else 1
    carry = carry_ref[...]
    init = carry
    for _ in range(SUBLANES - 1):
        init = jnp.where(row == edge, carry, pltpu.roll(f + pf * init, shift, 0))
    last = f + pf * init
    out_row = 0 if reverse else SUBLANES - 1
    carry_ref[...] = jnp.broadcast_to(last[out_row:out_row + 1, :], shape)

    def pass2(s, h):
        rows = step(s)
        h = a_ref[rows, :] * h + b_ref[rows, :]
        emit(rows, h)
        return h

    lax.fori_loop(0, sub, pass2, init, unroll=True)


def _inproj_body(x_ref, xp_ref, xn_ref, g_ref, w_ref, convw_ref, convb_ref, wg_ref, bias_ref, lam_ref,
                 up_ref, q_ref, gl_ref, af_ref, bf_ref, hb_ref,
                 ubuf, xq, qa, qb, carry_ref):
    i = pl.program_id(1)
    nc = pl.num_programs(1)
    ti = nc - 1 - i
    tc, d = x_ref.shape
    ncol = d // LANES
    sub = tc // SUBLANES
    gain = g_ref[...]

    hc = _rms(x_ref[...], gain)
    hcb = hc.astype(BF16)

    hall = jnp.concatenate([_rms(xp_ref[...], gain), hc, _rms(xn_ref[...], gain)], axis=0).astype(BF16)
    ul = jnp.dot(hall, w_ref[:, d:2 * d], preferred_element_type=F32)
    for c in range(ncol):
        cs = slice(c * LANES, (c + 1) * LANES)
        ubuf[c, pl.ds(0, HALO), :] = jnp.where(ti > 0, ul[0:HALO, cs], 0.0)
        ubuf[c, pl.ds(HALO, tc), :] = ul[HALO:HALO + tc, cs]
        ubuf[c, pl.ds(HALO + tc, HALO), :] = jnp.where(ti < nc - 1, ul[HALO + tc:, cs], 0.0)

    wide = [(up_ref, 0, 0, d), (q_ref, 0, 2 * d, d), (gl_ref, 0, 3 * d, 3 * d)]
    chunk = 4 * LANES
    wide_chunks = [(ref, o + k, col + k) for ref, o, col, n in wide for k in range(0, n, chunk)]

    def wide_dot(ref, o, col):
        r = jnp.dot(hcb, w_ref[:, col:col + chunk], preferred_element_type=F32)
        ref[:, o:o + chunk] = r.astype(ref.dtype)

    assert len(wide_chunks) >= ncol
    for c in range(ncol):
        cs = pl.ds(c * LANES, LANES)
        w = convw_ref[:, cs]
        xc = convb_ref[:, cs] + _shifted(ubuf, c, HALO - CONV_LEFT, tc) * w[0:1]
        for k in range(1, CONV_WIDTH):
            xc = xc + _shifted(ubuf, c, HALO - CONV_LEFT + k, tc) * w[k:k + 1]
        for j in range(SUBLANES):
            xq[c, pl.ds(j, sub, stride=SUBLANES), :] = xc[j * sub:(j + 1) * sub]
        wide_dot(*wide_chunks[c])

        xb = xq[c]
        g4 = jnp.dot(xb.astype(BF16), wg_ref[c], preferred_element_type=F32)
        for dirn in range(2):
            o = 2 * dirn * LANES
            tr = jnp.tanh(g4[:, o:o + LANES] + bias_ref[2 * dirn:2 * dirn + 1, cs])
            tx = jnp.tanh(g4[:, o + LANES:o + 2 * LANES] + bias_ref[2 * dirn + 1:2 * dirn + 2, cs])
            nk = (-0.5 * LRU_C) * jax.nn.softplus(-lam_ref[dirn:dirn + 1, cs])
            log_a = nk * tr + nk
            a = jnp.exp(log_a)
            m2 = jnp.tanh(log_a) * (-1.0 - a * a)
            mult = m2 * lax.rsqrt(jnp.maximum(m2, 1e-30))
            b = mult * (0.5 * tx + 0.5) * xb
            if dirn == 0:
                af_ref[:, cs] = a
                bf_ref[:, cs] = b
            else:
                qa[:, cs] = a
                qb[:, cs] = b

    def emit(rows, h):
        hb_ref[rows, :] = h

    rest = wide_chunks[ncol:]
    for args in rest[:len(rest) // 2]:
        wide_dot(*args)
    _scan(True, i == 0, qa, qb, carry_ref, emit)
    for args in rest[len(rest) // 2:]:
        wide_dot(*args)


def _inproj(x, gain, w_in, convw, convb, wg, bias, lam):
    b, s_len, d = x.shape
    tc = min(TIME_CHUNK, s_len)
    nc = s_len // tc
    ncol = d // LANES
    rev = lambda i: nc - 1 - i
    out = lambda w: pl.BlockSpec((None, tc, w), lambda bb, i: (bb, rev(i), 0))
    f32_out = lambda w: jax.ShapeDtypeStruct((b, s_len, w), F32)
    return pl.pallas_call(
        _inproj_body,
        grid=(b, nc),
        in_specs=_halo_specs(tc, d, s_len, rev) + [
            _const_spec(gain.shape), _resident_spec(w_in.shape), _const_spec(convw.shape),
            _const_spec(convb.shape), _const_spec(wg.shape), _const_spec(bias.shape), _const_spec(lam.shape)],
        out_specs=[out(d), out(d), out(3 * d), out(d), out(d), out(d)],
        out_shape=[f32_out(d), jax.ShapeDtypeStruct((b, s_len, d), BF16), f32_out(3 * d),
                   f32_out(d), f32_out(d), f32_out(d)],
        scratch_shapes=[pltpu.VMEM((ncol, tc + 2 * HALO, LANES), F32), pltpu.VMEM((ncol, tc, LANES), F32),
                        pltpu.VMEM((tc, d), F32), pltpu.VMEM((tc, d), F32), pltpu.VMEM((SUBLANES, d), F32)],
        compiler_params=pltpu.CompilerParams(dimension_semantics=("arbitrary", "arbitrary"),
                                             vmem_limit_bytes=VMEM_LIMIT),
        name="inproj",
    )(x, x, x, gain, w_in, convw, convb, wg, bias, lam)


def _memkv_body(scale, mem_ref, g_ref, wk_ref, wv_ref, kt_ref, v_ref):
    mn = _rms(mem_ref[...], g_ref[...]).astype(BF16)
    k = jnp.dot(mn, wk_ref[...], preferred_element_type=F32)
    kt_ref[...] = (k.T * scale).astype(BF16)
    v_ref[...] = jnp.dot(mn, wv_ref[...], preferred_element_type=F32).astype(BF16)


def _memkv(mem, gain, wk, wv):
    b, m, d = mem.shape
    scale = float((d // N_XHEADS) ** -0.5)
    return pl.pallas_call(
        functools.partial(_memkv_body, scale),
        grid=(b,),
        in_specs=[pl.BlockSpec((None, m, d), lambda i: (i, 0, 0)), _const_spec((1, d)),
                  _const_spec(wk.shape), _const_spec(wv.shape)],
        out_specs=[pl.BlockSpec((None, d, m), lambda i: (i, 0, 0)),
                   pl.BlockSpec((None, m, d), lambda i: (i, 0, 0))],
        out_shape=[jax.ShapeDtypeStruct((b, d, m), BF16), jax.ShapeDtypeStruct((b, m, d), BF16)],
        compiler_params=pltpu.CompilerParams(dimension_semantics=("arbitrary",),
                                             vmem_limit_bytes=VMEM_LIMIT),
        name="memkv",
    )(mem, gain, wk, wv)


def _fill_halo_buf(buf, cur_ref, prev_ref, next_ref, has_prev, has_next):
    tc = cur_ref.shape[0]
    ncol = cur_ref.shape[1] // LANES
    for c in range(ncol):
        cs = pl.ds(c * LANES, LANES)
        buf[c, pl.ds(0, HALO), :] = jnp.where(has_prev, prev_ref[:, cs], 0.0)
        buf[c, pl.ds(HALO, tc), :] = cur_ref[:, cs]
        buf[c, pl.ds(HALO + tc, HALO), :] = jnp.where(has_next, next_ref[:, cs], 0.0)
        buf[c, pl.ds(2 * HALO + tc, PAD_ROWS - 2 * HALO), :] = jnp.zeros((PAD_ROWS - 2 * HALO, LANES), F32)


def _unpermute(qh, col, tc):
    sub = tc // SUBLANES
    return jnp.concatenate(
        [qh[col, pl.ds(j, sub, stride=SUBLANES), :] for j in range(SUBLANES)], axis=0)


def _window_sums(pbuf, t1, t2, group, tc):
    outs = []
    for c in (2 * group, 2 * group + 1):
        if group == 0:
            outs.append(_shifted(pbuf, c, 7, tc) + pbuf[c, pl.ds(8, tc), :])
        elif group == 1:
            outs.append((_shifted(pbuf, c, 6, tc) + _shifted(pbuf, c, 7, tc))
                        + (pbuf[c, pl.ds(8, tc), :] + _shifted(pbuf, c, 9, tc)))
        else:
            n2 = tc + 24 if group == 3 else tc + 16
            t1[0, pl.ds(0, n2), :] = pbuf[c, pl.ds(0, n2), :] + _shifted(pbuf, c, 1, n2)
            n4 = n2 - 8
            t2[0, pl.ds(0, n4), :] = t1[0, pl.ds(0, n4), :] + _shifted(t1, 0, 2, n4)
            if group == 2:
                outs.append(_shifted(t2, 0, 4, tc) + t2[0, pl.ds(8, tc), :])
            else:
                n8 = n4 - 8
                t1[0, pl.ds(0, n8), :] = t2[0, pl.ds(0, n8), :] + _shifted(t2, 0, 4, n8)
                outs.append(t1[0, pl.ds(0, tc), :] + t1[0, pl.ds(8, tc), :])
    return outs


def _mixer_body(s_len, x_ref, up_ref, upp_ref, upn_ref, af_ref, bf_ref, hb_ref, q_ref, gl_ref,
                kt_ref, v_ref, poolw_ref, pscale_ref, wout_ref, gpost_ref, o_ref,
                pbuf, qh, carry_ref, t1, t2, mbuf):
    i = pl.program_id(1)
    nc = pl.num_programs(1)
    tc, d = x_ref.shape
    ncol = d // LANES
    _fill_halo_buf(pbuf, up_ref, upp_ref, upn_ref, i > 0, i < nc - 1)

    def emit(rows, h):
        y = h + hb_ref[rows, :]
        for c in range(ncol):
            qh[c, rows, :] = y[:, c * LANES:(c + 1) * LANES]

    _scan(False, i == 0, af_ref, bf_ref, carry_ref, emit)

    t_glob = i * tc + lax.broadcasted_iota(jnp.int32, (tc, LANES), 0)
    gw = d // N_POOL_GROUPS
    for g in range(N_POOL_GROUPS):
        cols = pl.ds(g * gw, gw)
        half = POOL_WINDOWS[g] // 2
        count = jnp.minimum(t_glob + half, s_len) - jnp.maximum(t_glob - half, 0)
        inv = 1.0 / count.astype(F32)
        sums = _window_sums(pbuf, t1, t2, g, tc)
        pooled = jnp.concatenate(
            [sums[k] * inv - pbuf[2 * g + k, pl.ds(HALO, tc), :] for k in range(2)], axis=1)
        yp = jnp.dot(pooled.astype(BF16), poolw_ref[g], preferred_element_type=F32) * pscale_ref[:, cols]

        sc = jnp.dot(q_ref[:, cols], kt_ref[cols, :], preferred_element_type=F32)
        e = jnp.exp(sc - jnp.max(sc, axis=-1, keepdims=True))
        p = e * (1.0 / jnp.sum(e, axis=-1, keepdims=True))
        ym = jnp.dot(p.astype(BF16), v_ref[:, cols], preferred_element_type=F32)

        yl = jnp.concatenate([_unpermute(qh, 2 * g, tc), _unpermute(qh, 2 * g + 1, tc)], axis=1)

        merged = (yp + yl + ym) + jnp.tanh(gl_ref[:, cols]) * yp \
            + jnp.tanh(gl_ref[:, pl.ds(d + g * gw, gw)]) * yl \
            + jnp.tanh(gl_ref[:, pl.ds(2 * d + g * gw, gw)]) * ym
        mbuf[:, cols] = merged.astype(BF16)

    out = jnp.dot(mbuf[...], wout_ref[...], preferred_element_type=F32)
    o_ref[...] = x_ref[...] + _rms(out, gpost_ref[...])


def _mixer(x, up, af, bf, hb, q, gl, kt, v, poolw, pscale, wout, gpost):
    b, s_len, d = x.shape
    tc = min(TIME_CHUNK, s_len)
    nc = s_len // tc
    ncol = d // LANES
    fwd = lambda i: i
    cur = lambda w: pl.BlockSpec((None, tc, w), lambda bb, i: (bb, i, 0))
    per_batch = lambda a: pl.BlockSpec((None,) + a.shape[1:], lambda bb, i: (bb, 0, 0))
    consts = [poolw, pscale, wout, gpost]
    scratch = [pltpu.VMEM((ncol, tc + PAD_ROWS, LANES), F32), pltpu.VMEM((ncol, tc, LANES), F32),
               pltpu.VMEM((SUBLANES, d), F32),
               pltpu.VMEM((1, tc + PAD_ROWS, LANES), F32), pltpu.VMEM((1, tc + PAD_ROWS, LANES), F32),
               pltpu.VMEM((tc, d), BF16)]
    return pl.pallas_call(
        functools.partial(_mixer_body, s_len),
        grid=(b, nc),
        in_specs=[cur(d)] + _halo_specs(tc, d, s_len, fwd)
                 + [cur(d), cur(d), cur(d), cur(d), cur(3 * d), per_batch(kt), per_batch(v)]
                 + [_const_spec(a.shape) for a in consts],
        out_specs=cur(d),
        out_shape=jax.ShapeDtypeStruct((b, s_len, d), F32),
        scratch_shapes=scratch,
        compiler_params=pltpu.CompilerParams(dimension_semantics=("arbitrary", "arbitrary"),
                                             vmem_limit_bytes=VMEM_LIMIT),
        name="mixer",
    )(x, up, up, up, af, bf, hb, q, gl, kt, v, *consts)


def _mlp_body(x_ref, gpre_ref, gpost_ref, w1_ref, w2_ref, o_ref):
    d = x_ref.shape[1]
    x = x_ref[...]
    h = _rms(x, gpre_ref[...]).astype(BF16)
    y = None
    for f in range(w1_ref.shape[1] // d):
        a = jnp.maximum(jnp.dot(h, w1_ref[:, f * d:(f + 1) * d], preferred_element_type=F32), 0.0)
        part = jnp.dot((a * a).astype(BF16), w2_ref[f * d:(f + 1) * d, :], preferred_element_type=F32)
        y = part if y is None else y + part
    o_ref[...] = x + _rms(y, gpost_ref[...])


def _mlp(x2, gpre, gpost, w1, w2):
    t, d = x2.shape
    tm = min(TOKEN_TILE, t)
    row = lambda i: (i, 0)
    return pl.pallas_call(
        _mlp_body,
        grid=(t // tm,),
        in_specs=[pl.BlockSpec((tm, d), row), _const_spec((1, d)), _const_spec((1, d)),
                  _resident_spec(w1.shape), _resident_spec(w2.shape)],
        out_specs=pl.BlockSpec((tm, d), row),
        out_shape=jax.ShapeDtypeStruct((t, d), F32),
        compiler_params=pltpu.CompilerParams(dimension_semantics=("arbitrary",),
                                             vmem_limit_bytes=VMEM_LIMIT),
        name="mlp",
    )(x2, gpre, gpost, w1, w2)


def _prep_layer(l, p):
    d = p["w_in"].shape[1]
    row = lambda a: a.reshape(1, -1)
    w_in = p["w_in"][l]
    w_in = jnp.concatenate([w_in[:, :3 * d], 0.5 * w_in[:, 3 * d:]], axis=1).astype(BF16)
    wg = (0.5 * jnp.concatenate([p["lru_wa"][l, 0], p["lru_wx"][l, 0],
                                 p["lru_wa"][l, 1], p["lru_wx"][l, 1]], axis=-1)).astype(BF16)
    bias = 0.5 * jnp.stack([p["lru_ba"][l, 0], p["lru_bx"][l, 0], p["lru_ba"][l, 1], p["lru_bx"][l, 1]])
    return dict(
        g_pre=row(p["norm_mix_pre"][l]), g_post=row(p["norm_mix_post"][l]), g_mem=row(p["norm_mem"][l]),
        w_in=w_in, pool_w=p["pool_w"][l].astype(BF16), pool_scale=row(p["pool_scale"][l]),
        conv_w=p["conv_w"][l], conv_b=row(p["conv_b"][l]), wg=wg, bias=bias, lam=p["lru_lambda"][l],
        wk=p["w_kv"][l][:, :d].astype(BF16), wv=p["w_kv"][l][:, d:].astype(BF16),
        w_out=(0.5 * p["w_out"][l]).astype(BF16),
        m_pre=row(p["norm_mlp_pre"][l]), m_post=row(p["norm_mlp_post"][l]),
        w1=p["mlp_w1"][l].astype(BF16), w2=p["mlp_w2"][l].astype(BF16),
    )


def _trunk(x, mem, layers):
    b, s_len, d = x.shape
    for lp in layers:
        up, q, gl, af, bf, hb = _inproj(x, lp["g_pre"], lp["w_in"], lp["conv_w"], lp["conv_b"],
                                        lp["wg"], lp["bias"], lp["lam"])
        kt, v = _memkv(mem, lp["g_mem"], lp["wk"], lp["wv"])
        x = _mixer(x, up, af, bf, hb, q, gl, kt, v, lp["pool_w"], lp["pool_scale"], lp["w_out"], lp["g_post"])
        x = _mlp(x.reshape(b * s_len, d), lp["m_pre"], lp["m_post"], lp["w1"], lp["w2"]).reshape(b, s_len, d)
    return x


def kernel(x_prompt, x_sample, mem_prompt, mem_sample, norm_mix_pre, norm_mix_post, norm_mem, w_in, pool_w, pool_scale, conv_w, conv_b, lru_wa, lru_ba, lru_wx, lru_bx, lru_lambda, w_kv, w_out, norm_mlp_pre, norm_mlp_post, mlp_w1, mlp_w2):
    p = dict(norm_mix_pre=norm_mix_pre, norm_mix_post=norm_mix_post, norm_mem=norm_mem, w_in=w_in,
             pool_w=pool_w, pool_scale=pool_scale, conv_w=conv_w, conv_b=conv_b, lru_wa=lru_wa,
             lru_ba=lru_ba, lru_wx=lru_wx, lru_bx=lru_bx, lru_lambda=lru_lambda, w_kv=w_kv, w_out=w_out,
             norm_mlp_pre=norm_mlp_pre, norm_mlp_post=norm_mlp_post, mlp_w1=mlp_w1, mlp_w2=mlp_w2)
    layers = [_prep_layer(l, p) for l in range(w_in.shape[0])]
    return (_trunk(x_prompt, mem_prompt, layers), _trunk(x_sample, mem_sample, layers))
```

```python
import functools

import jax
import jax.numpy as jnp
from jax import lax
from jax.experimental import pallas as pl
from jax.experimental.pallas import tpu as pltpu

F32 = jnp.float32
BF16 = jnp.bfloat16

LANES = 128
SUBLANES = 8
EPS = 1e-6
LRU_C = 8.0
POOL_WINDOWS = (2, 4, 8, 16)
N_POOL_GROUPS = len(POOL_WINDOWS)
N_XHEADS = 4
CONV_WIDTH = 4
CONV_LEFT = 2
HALO = 8
PAD_ROWS = 32
VMEM_LIMIT = 56 * 1024 * 1024

TIME_CHUNK = 256


def _rms(x, gain):
    ms = jnp.mean(x * x, axis=-1, keepdims=True)
    return x * lax.rsqrt(ms + EPS) * gain


def _const_spec(shape):
    nd = len(shape)
    return pl.BlockSpec(shape, lambda *_: (0,) * nd)


def _resident_spec(shape):
    nd = len(shape)
    return pl.BlockSpec(shape, lambda *_: (0,) * nd, pipeline_mode=pl.Buffered(1))


def _shifted(buf, col, start, rows):
    return buf[pl.ds(col, 1, stride=2), pl.ds(start, rows), :].reshape(rows, LANES)


def _halo_specs(tc, d, s_len, chunk_of):
    per = tc // HALO
    last = s_len // HALO - 1
    cur = pl.BlockSpec((None, tc, d), lambda b, i: (b, chunk_of(i), 0))
    prev = pl.BlockSpec((None, HALO, d), lambda b, i: (b, jnp.maximum(chunk_of(i) * per - 1, 0), 0))
    nxt = pl.BlockSpec((None, HALO, d), lambda b, i: (b, jnp.minimum((chunk_of(i) + 1) * per, last), 0))
    return [cur, prev, nxt]


def _scan(reverse, first_step, a_ref, b_ref, carry_ref, emit, tick=lambda: None, tick_every=8):
    tc, d = a_ref.shape
    sub = tc // SUBLANES

    @pl.when(first_step)
    def _():
        carry_ref[...] = jnp.zeros(carry_ref.shape, F32)

    def step(s):
        k = (sub - 1 - s) if reverse else s
        return pl.ds(k * SUBLANES, SUBLANES)

    shape = (SUBLANES, d)
    f, pf = jnp.zeros(shape, F32), jnp.ones(shape, F32)
    for s in range(sub):
        rows = step(s)
        a = a_ref[rows, :]
        f, pf = a * f + b_ref[rows, :], a * pf
        if (s + 1) % tick_every == 0:
            tick()

    row = lax.broadcasted_iota(jnp.int32, shape, 0)
    edge = (SUBLANES - 1) if reverse else 0
    shift = (SUBLANES - 1) if reverse else 1
    carry = carry_ref[...]
    init = carry
    for _ in range(SUBLANES - 1):
        init = jnp.where(row == edge, carry, pltpu.roll(f + pf * init, shift, 0))
    last = f + pf * init
    out_row = 0 if reverse else SUBLANES - 1
    carry_ref[...] = jnp.broadcast_to(last[out_row:out_row + 1, :], shape)
    tick()

    h = init
    for s in range(sub):
        rows = step(s)
        h = a_ref[rows, :] * h + b_ref[rows, :]
        emit(rows, h)
        if (s + 1) % tick_every == 0:
            tick()


def _inproj_body(x_ref, xp_ref, xn_ref, g_ref, w_ref, convw_ref, convb_ref, wg_ref, bias_ref, lam_ref,
                 up_ref, q_ref, gl_ref, af_ref, bf_ref, hb_ref,
                 ubuf, xq, qa, qb, carry_ref):
    i = pl.program_id(1)
    nc = pl.num_programs(1)
    ti = nc - 1 - i
    tc, d = x_ref.shape
    ncol = d // LANES
    sub = tc // SUBLANES
    gain = g_ref[...]

    hc = _rms(x_ref[...], gain)
    hcb = hc.astype(BF16)

    hall = jnp.concatenate([_rms(xp_ref[...], gain), hc, _rms(xn_ref[...], gain)], axis=0).astype(BF16)
    ul = jnp.dot(hall, w_ref[:, d:2 * d], preferred_element_type=F32)
    for c in range(ncol):
        cs = slice(c * LANES, (c + 1) * LANES)
        ubuf[c, pl.ds(0, HALO), :] = jnp.where(ti > 0, ul[0:HALO, cs], 0.0)
        ubuf[c, pl.ds(HALO, tc), :] = ul[HALO:HALO + tc, cs]
        ubuf[c, pl.ds(HALO + tc, HALO), :] = jnp.where(ti < nc - 1, ul[HALO + tc:, cs], 0.0)

    wide = [(up_ref, 0, 0, d), (q_ref, 0, 2 * d, d), (gl_ref, 0, 3 * d, 3 * d)]
    chunk = 4 * LANES
    wide_chunks = [(ref, o + k, col + k) for ref, o, col, n in wide for k in range(0, n, chunk)]

    def wide_dot(ref, o, col):
        r = jnp.dot(hcb, w_ref[:, col:col + chunk], preferred_element_type=F32)
        ref[:, o:o + chunk] = r.astype(ref.dtype)

    assert len(wide_chunks) >= ncol
    for c in range(ncol):
        cs = pl.ds(c * LANES, LANES)
        w = convw_ref[:, cs]
        xc = convb_ref[:, cs] + _shifted(ubuf, c, HALO - CONV_LEFT, tc) * w[0:1]
        for k in range(1, CONV_WIDTH):
            xc = xc + _shifted(ubuf, c, HALO - CONV_LEFT + k, tc) * w[k:k + 1]
        for j in range(SUBLANES):
            xq[c, pl.ds(j, sub, stride=SUBLANES), :] = xc[j * sub:(j + 1) * sub]
        wide_dot(*wide_chunks[c])

        xb = xq[c]
        g4 = jnp.dot(xb.astype(BF16), wg_ref[c], preferred_element_type=F32)
        for dirn in range(2):
            o = 2 * dirn * LANES
            tr = jnp.tanh(g4[:, o:o + LANES] + bias_ref[2 * dirn:2 * dirn + 1, cs])
            tx = jnp.tanh(g4[:, o + LANES:o + 2 * LANES] + bias_ref[2 * dirn + 1:2 * dirn + 2, cs])
            nk = (-0.5 * LRU_C) * jax.nn.softplus(-lam_ref[dirn:dirn + 1, cs])
            log_a = nk * tr + nk
            a = jnp.exp(log_a)
            m2 = jnp.tanh(log_a) * (-1.0 - a * a)
            mult = m2 * lax.rsqrt(jnp.maximum(m2, 1e-30))
            b = mult * (0.5 * tx + 0.5) * xb
            if dirn == 0:
                af_ref[:, cs] = a
                bf_ref[:, cs] = b
            else:
                qa[:, cs] = a
                qb[:, cs] = b

    def emit(rows, h):
        hb_ref[rows, :] = h

    rest = wide_chunks[ncol:]
    for args in rest[:len(rest) // 2]:
        wide_dot(*args)
    _scan(True, i == 0, qa, qb, carry_ref, emit)
    for args in rest[len(rest) // 2:]:
        wide_dot(*args)


def _inproj(x, gain, w_in, convw, convb, wg, bias, lam):
    b, s_len, d = x.shape
    tc = min(TIME_CHUNK, s_len)
    nc = s_len // tc
    ncol = d // LANES
    rev = lambda i: nc - 1 - i
    out = lambda w: pl.BlockSpec((None, tc, w), lambda bb, i: (bb, rev(i), 0))
    f32_out = lambda w: jax.ShapeDtypeStruct((b, s_len, w), F32)
    return pl.pallas_call(
        _inproj_body,
        grid=(b, nc),
        in_specs=_halo_specs(tc, d, s_len, rev) + [
            _const_spec(gain.shape), _resident_spec(w_in.shape), _const_spec(convw.shape),
            _const_spec(convb.shape), _const_spec(wg.shape), _const_spec(bias.shape), _const_spec(lam.shape)],
        out_specs=[out(d), out(d), out(3 * d), out(d), out(d), out(d)],
        out_shape=[f32_out(d), jax.ShapeDtypeStruct((b, s_len, d), BF16), f32_out(3 * d),
                   f32_out(d), f32_out(d), f32_out(d)],
        scratch_shapes=[pltpu.VMEM((ncol, tc + 2 * HALO, LANES), F32), pltpu.VMEM((ncol, tc, LANES), F32),
                        pltpu.VMEM((tc, d), F32), pltpu.VMEM((tc, d), F32), pltpu.VMEM((SUBLANES, d), F32)],
        compiler_params=pltpu.CompilerParams(dimension_semantics=("arbitrary", "arbitrary"),
                                             vmem_limit_bytes=VMEM_LIMIT),
        name="inproj",
    )(x, x, x, gain, w_in, convw, convb, wg, bias, lam)


def _memkv_body(scale, mem_ref, g_ref, wk_ref, wv_ref, kt_ref, v_ref):
    mn = _rms(mem_ref[...], g_ref[...]).astype(BF16)
    k = jnp.dot(mn, wk_ref[...], preferred_element_type=F32)
    kt_ref[...] = (k.T * scale).astype(BF16)
    v_ref[...] = jnp.dot(mn, wv_ref[...], preferred_element_type=F32).astype(BF16)


def _memkv(mem, gain, wk, wv):
    b, m, d = mem.shape
    scale = float((d // N_XHEADS) ** -0.5)
    return pl.pallas_call(
        functools.partial(_memkv_body, scale),
        grid=(b,),
        in_specs=[pl.BlockSpec((None, m, d), lambda i: (i, 0, 0)), _const_spec((1, d)),
                  _const_spec(wk.shape), _const_spec(wv.shape)],
        out_specs=[pl.BlockSpec((None, d, m), lambda i: (i, 0, 0)),
                   pl.BlockSpec((None, m, d), lambda i: (i, 0, 0))],
        out_shape=[jax.ShapeDtypeStruct((b, d, m), BF16), jax.ShapeDtypeStruct((b, m, d), BF16)],
        compiler_params=pltpu.CompilerParams(dimension_semantics=("arbitrary",),
                                             vmem_limit_bytes=VMEM_LIMIT),
        name="memkv",
    )(mem, gain, wk, wv)


def _fill_halo_buf(buf, cur_ref, prev_ref, next_ref, has_prev, has_next):
    tc = cur_ref.shape[0]
    ncol = cur_ref.shape[1] // LANES
    for c in range(ncol):
        cs = pl.ds(c * LANES, LANES)
        buf[c, pl.ds(0, HALO), :] = jnp.where(has_prev, prev_ref[:, cs], 0.0)
        buf[c, pl.ds(HALO, tc), :] = cur_ref[:, cs]
        buf[c, pl.ds(HALO + tc, HALO), :] = jnp.where(has_next, next_ref[:, cs], 0.0)
        buf[c, pl.ds(2 * HALO + tc, PAD_ROWS - 2 * HALO), :] = jnp.zeros((PAD_ROWS - 2 * HALO, LANES), F32)


def _unpermute(qh, col, tc):
    sub = tc // SUBLANES
    return jnp.concatenate(
        [qh[col, pl.ds(j, sub, stride=SUBLANES), :] for j in range(SUBLANES)], axis=0)


def _window_sums(pbuf, t1, t2, group, tc):
    outs = []
    for c in (2 * group, 2 * group + 1):
        if group == 0:
            outs.append(_shifted(pbuf, c, 7, tc) + pbuf[c, pl.ds(8, tc), :])
        elif group == 1:
            outs.append((_shifted(pbuf, c, 6, tc) + _shifted(pbuf, c, 7, tc))
                        + (pbuf[c, pl.ds(8, tc), :] + _shifted(pbuf, c, 9, tc)))
        else:
            n2 = tc + 24 if group == 3 else tc + 16
            t1[0, pl.ds(0, n2), :] = pbuf[c, pl.ds(0, n2), :] + _shifted(pbuf, c, 1, n2)
            n4 = n2 - 8
            t2[0, pl.ds(0, n4), :] = t1[0, pl.ds(0, n4), :] + _shifted(t1, 0, 2, n4)
            if group == 2:
                outs.append(_shifted(t2, 0, 4, tc) + t2[0, pl.ds(8, tc), :])
            else:
                n8 = n4 - 8
                t1[0, pl.ds(0, n8), :] = t2[0, pl.ds(0, n8), :] + _shifted(t2, 0, 4, n8)
                outs.append(t1[0, pl.ds(0, tc), :] + t1[0, pl.ds(8, tc), :])
    return outs


def _mixmlp_body(s_len, nc, x_ref, up_ref, upp_ref, upn_ref, af_ref, bf_ref, hb_ref, q_ref, gl_ref,
                 kt_ref, v_ref, poolw_ref, pscale_ref, wout_ref, gpost_ref, mpre_ref, mpost_ref,
                 w1_ref, w2_ref, o_ref,
                 pbuf, qh, carry_ref, t1, t2, mbuf, xmid, hbuf, a2buf, yacc):
    s = pl.program_id(0)
    last = pl.num_programs(0) - 2
    i = jnp.minimum(s, last) % nc
    tc, d = x_ref.shape
    ncol = d // LANES
    n_ff = w1_ref.shape[1] // d

    @pl.when(s == 0)
    def _():
        xmid[...] = jnp.zeros(xmid.shape, F32)
        hbuf[...] = jnp.zeros(hbuf.shape, BF16)

    pw = 2 * LANES
    n_piece = d // pw

    def up_piece(f, n):
        def run():
            a = jnp.maximum(jnp.dot(hbuf[...], w1_ref[:, f * d + n * pw:f * d + (n + 1) * pw],
                                    preferred_element_type=F32), 0.0)
            a2buf[f % 2, :, n * pw:(n + 1) * pw] = (a * a).astype(BF16)
        return run

    def down_piece(f, n):
        def run():
            part = jnp.dot(a2buf[f % 2], w2_ref[f * d:(f + 1) * d, n * pw:(n + 1) * pw],
                           preferred_element_type=F32)
            if f == 0:
                yacc[:, n * pw:(n + 1) * pw] = part
            else:
                yacc[:, n * pw:(n + 1) * pw] += part
        return run

    queue = [up_piece(0, n) for n in range(n_piece)]
    for f in range(n_ff):
        for n in range(n_piece):
            if f + 1 < n_ff:
                queue.append(up_piece(f + 1, n))
            queue.append(down_piece(f, n))
    queue.reverse()

    def mlp_step(k=1):
        for _ in range(k):
            if queue:
                queue.pop()()

    def mlp_tail():
        o_ref[...] = xmid[(s + 1) % 2] + _rms(yacc[...], mpost_ref[...])

    mlp_step()
    _fill_halo_buf(pbuf, up_ref, upp_ref, upn_ref, i > 0, i < nc - 1)
    mlp_step()

    def emit(rows, h):
        y = h + hb_ref[rows, :]
        for c in range(ncol):
            qh[c, rows, :] = y[:, c * LANES:(c + 1) * LANES]

    _scan(False, i == 0, af_ref, bf_ref, carry_ref, emit, tick=mlp_step)

    t_glob = i * tc + lax.broadcasted_iota(jnp.int32, (tc, LANES), 0)
    gw = d // N_POOL_GROUPS
    for g in range(N_POOL_GROUPS):
        cols = pl.ds(g * gw, gw)
        half = POOL_WINDOWS[g] // 2
        count = jnp.minimum(t_glob + half, s_len) - jnp.maximum(t_glob - half, 0)
        inv = 1.0 / count.astype(F32)
        sums = _window_sums(pbuf, t1, t2, g, tc)
        pooled = jnp.concatenate(
            [sums[k] * inv - pbuf[2 * g + k, pl.ds(HALO, tc), :] for k in range(2)], axis=1)
        mlp_step()
        yp = jnp.dot(pooled.astype(BF16), poolw_ref[g], preferred_element_type=F32) * pscale_ref[:, cols]
        mlp_step()

        sc = jnp.dot(q_ref[:, cols], kt_ref[cols, :], preferred_element_type=F32)
        e = jnp.exp(sc - jnp.max(sc, axis=-1, keepdims=True))
        p = e * (1.0 / jnp.sum(e, axis=-1, keepdims=True))
        mlp_step()
        ym = jnp.dot(p.astype(BF16), v_ref[:, cols], preferred_element_type=F32)
        mlp_step()

        yl = jnp.concatenate([_unpermute(qh, 2 * g, tc), _unpermute(qh, 2 * g + 1, tc)], axis=1)

        merged = (yp + yl + ym) + jnp.tanh(gl_ref[:, cols]) * yp \
            + jnp.tanh(gl_ref[:, pl.ds(d + g * gw, gw)]) * yl \
            + jnp.tanh(gl_ref[:, pl.ds(2 * d + g * gw, gw)]) * ym
        mbuf[:, cols] = merged.astype(BF16)
        mlp_step()

    mlp_step(len(queue))
    out = jnp.dot(mbuf[...], wout_ref[...], preferred_element_type=F32)
    mlp_tail()
    xnew = x_ref[...] + _rms(out, gpost_ref[...])
    xmid[s % 2] = xnew
    hbuf[...] = _rms(xnew, mpre_ref[...]).astype(BF16)


def _mixmlp(x, up, af, bf, hb, q, gl, kt, v, poolw, pscale, wout, gpost, mpre, mpost, w1, w2):
    b, s_len, d = x.shape
    tc = min(TIME_CHUNK, s_len)
    nc = s_len // tc
    total = b * nc
    ncol = d // LANES
    per = tc // HALO
    last_halo = s_len // HALO - 1
    assert w1.shape[1] // d >= N_POOL_GROUPS

    def mix(s):
        c = jnp.minimum(s, total - 1)
        return c // nc, c % nc

    def cur(w):
        return pl.BlockSpec((None, tc, w), lambda s: (*mix(s), 0))

    prev = pl.BlockSpec((None, HALO, d), lambda s: (mix(s)[0], jnp.maximum(mix(s)[1] * per - 1, 0), 0))
    nxt = pl.BlockSpec((None, HALO, d), lambda s: (mix(s)[0], jnp.minimum((mix(s)[1] + 1) * per, last_halo), 0))
    per_batch = lambda a: pl.BlockSpec((None,) + a.shape[1:], lambda s: (mix(s)[0], 0, 0))

    def out_map(s):
        c = jnp.maximum(s - 1, 0)
        return c // nc, c % nc, 0

    scratch = [pltpu.VMEM((ncol, tc + PAD_ROWS, LANES), F32), pltpu.VMEM((ncol, tc, LANES), F32),
               pltpu.VMEM((SUBLANES, d), F32),
               pltpu.VMEM((1, tc + PAD_ROWS, LANES), F32), pltpu.VMEM((1, tc + PAD_ROWS, LANES), F32),
               pltpu.VMEM((tc, d), BF16), pltpu.VMEM((2, tc, d), F32),
               pltpu.VMEM((tc, d), BF16), pltpu.VMEM((2, tc, d), BF16), pltpu.VMEM((tc, d), F32)]
    return pl.pallas_call(
        functools.partial(_mixmlp_body, s_len, nc),
        grid=(total + 1,),
        in_specs=[cur(d), cur(d), prev, nxt, cur(d), cur(d), cur(d), cur(d), cur(3 * d),
                  per_batch(kt), per_batch(v),
                  _resident_spec(poolw.shape), _const_spec(pscale.shape), _resident_spec(wout.shape),
                  _const_spec(gpost.shape), _const_spec(mpre.shape), _const_spec(mpost.shape),
                  _resident_spec(w1.shape), _resident_spec(w2.shape)],
        out_specs=pl.BlockSpec((None, tc, d), out_map),
        out_shape=jax.ShapeDtypeStruct((b, s_len, d), F32),
        scratch_shapes=scratch,
        compiler_params=pltpu.CompilerParams(dimension_semantics=("arbitrary",),
                                             vmem_limit_bytes=VMEM_LIMIT),
        name="mixmlp",
    )(x, up, up, up, af, bf, hb, q, gl, kt, v, poolw, pscale, wout, gpost, mpre, mpost, w1, w2)


def _prep_layer(l, p):
    d = p["w_in"].shape[1]
    row = lambda a: a.reshape(1, -1)
    w_in = p["w_in"][l]
    w_in = jnp.concatenate([w_in[:, :3 * d], 0.5 * w_in[:, 3 * d:]], axis=1).astype(BF16)
    wg = (0.5 * jnp.concatenate([p["lru_wa"][l, 0], p["lru_wx"][l, 0],
                                 p["lru_wa"][l, 1], p["lru_wx"][l, 1]], axis=-1)).astype(BF16)
    bias = 0.5 * jnp.stack([p["lru_ba"][l, 0], p["lru_bx"][l, 0], p["lru_ba"][l, 1], p["lru_bx"][l, 1]])
    return dict(
        g_pre=row(p["norm_mix_pre"][l]), g_post=row(p["norm_mix_post"][l]), g_mem=row(p["norm_mem"][l]),
        w_in=w_in, pool_w=p["pool_w"][l].astype(BF16), pool_scale=row(p["pool_scale"][l]),
        conv_w=p["conv_w"][l], conv_b=row(p["conv_b"][l]), wg=wg, bias=bias, lam=p["lru_lambda"][l],
        wk=p["w_kv"][l][:, :d].astype(BF16), wv=p["w_kv"][l][:, d:].astype(BF16),
        w_out=(0.5 * p["w_out"][l]).astype(BF16),
        m_pre=row(p["norm_mlp_pre"][l]), m_post=row(p["norm_mlp_post"][l]),
        w1=p["mlp_w1"][l].astype(BF16), w2=p["mlp_w2"][l].astype(BF16),
    )


def _trunk(x, mem, layers):
    b, s_len, d = x.shape
    for lp in layers:
        up, q, gl, af, bf, hb = _inproj(x, lp["g_pre"], lp["w_in"], lp["conv_w"], lp["conv_b"],
                                        lp["wg"], lp["bias"], lp["lam"])
        kt, v = _memkv(mem, lp["g_mem"], lp["wk"], lp["wv"])
        x = _mixmlp(x, up, af, bf, hb, q, gl, kt, v, lp["pool_w"], lp["pool_scale"], lp["w_out"], lp["g_post"],
                    lp["m_pre"], lp["m_post"], lp["w1"], lp["w2"])
    return x


def kernel(x_prompt, x_sample, mem_prompt, mem_sample, norm_mix_pre, norm_mix_post, norm_mem, w_in, pool_w, pool_scale, conv_w, conv_b, lru_wa, lru_ba, lru_wx, lru_bx, lru_lambda, w_kv, w_out, norm_mlp_pre, norm_mlp_post, mlp_w1, mlp_w2):
    p = dict(norm_mix_pre=norm_mix_pre, norm_mix_post=norm_mix_post, norm_mem=norm_mem, w_in=w_in,
             pool_w=pool_w, pool_scale=pool_scale, conv_w=conv_w, conv_b=conv_b, lru_wa=lru_wa,
             lru_ba=lru_ba, lru_wx=lru_wx, lru_bx=lru_bx, lru_lambda=lru_lambda, w_kv=w_kv, w_out=w_out,
             norm_mlp_pre=norm_mlp_pre, norm_mlp_post=norm_mlp_post, mlp_w1=mlp_w1, mlp_w2=mlp_w2)
    layers = [_prep_layer(l, p) for l in range(w_in.shape[0])]
    return (_trunk(x_prompt, mem_prompt, layers), _trunk(x_sample, mem_sample, layers))
```

```python
import functools

import jax
import jax.numpy as jnp
from jax import lax
from jax.experimental import pallas as pl
from jax.experimental.pallas import tpu as pltpu

F32 = jnp.float32
BF16 = jnp.bfloat16

LANES = 128
SUBLANES = 8
EPS = 1e-6
LRU_C = 8.0
POOL_WINDOWS = (2, 4, 8, 16)
N_POOL_GROUPS = len(POOL_WINDOWS)
N_XHEADS = 4
CONV_WIDTH = 4
CONV_LEFT = 2
HALO = 8
PAD_ROWS = 32
VMEM_LIMIT = 56 * 1024 * 1024

TIME_CHUNK = 256


def _rms(x, gain):
    ms = jnp.mean(x * x, axis=-1, keepdims=True)
    return x * lax.rsqrt(ms + EPS) * gain


def _const_spec(shape):
    nd = len(shape)
    return pl.BlockSpec(shape, lambda *_: (0,) * nd)


def _resident_spec(shape):
    nd = len(shape)
    return pl.BlockSpec(shape, lambda *_: (0,) * nd, pipeline_mode=pl.Buffered(1))


def _shifted(buf, col, start, rows):
    return buf[pl.ds(col, 1, stride=2), pl.ds(start, rows), :].reshape(rows, LANES)


def _halo_specs(tc, d, s_len, chunk_of):
    per = tc // HALO
    last = s_len // HALO - 1
    cur = pl.BlockSpec((None, tc, d), lambda b, i: (b, chunk_of(i), 0))
    prev = pl.BlockSpec((None, HALO, d), lambda b, i: (b, jnp.maximum(chunk_of(i) * per - 1, 0), 0))
    nxt = pl.BlockSpec((None, HALO, d), lambda b, i: (b, jnp.minimum((chunk_of(i) + 1) * per, last), 0))
    return [cur, prev, nxt]


def _scan(reverse, first_step, a_ref, b_ref, carry_ref, emit, tick=lambda: None, tick_every=8):
    tc, d = a_ref.shape
    sub = tc // SUBLANES

    @pl.when(first_step)
    def _():
        carry_ref[...] = jnp.zeros(carry_ref.shape, F32)

    def step(s):
        k = (sub - 1 - s) if reverse else s
        return pl.ds(k * SUBLANES, SUBLANES)

    shape = (SUBLANES, d)
    f, pf = jnp.zeros(shape, F32), jnp.ones(shape, F32)
    for s in range(sub):
        rows = step(s)
        a = a_ref[rows, :]
        f, pf = a * f + b_ref[rows, :], a * pf
        if (s + 1) % tick_every == 0:
            tick()

    row = lax.broadcasted_iota(jnp.int32, shape, 0)
    edge = (SUBLANES - 1) if reverse else 0
    shift = (SUBLANES - 1) if reverse else 1
    carry = carry_ref[...]
    init = carry
    for _ in range(SUBLANES - 1):
        init = jnp.where(row == edge, carry, pltpu.roll(f + pf * init, shift, 0))
    last = f + pf * init
    out_row = 0 if reverse else SUBLANES - 1
    carry_ref[...] = jnp.broadcast_to(last[out_row:out_row + 1, :], shape)
    tick()

    h = init
    for s in range(sub):
        rows = step(s)
        h = a_ref[rows, :] * h + b_ref[rows, :]
        emit(rows, h)
        if (s + 1) % tick_every == 0:
            tick()


def _inproj_body(x_ref, xp_ref, xn_ref, g_ref, w_ref, convw_ref, convb_ref, wg_ref, bias_ref, lam_ref,
                 up_ref, q_ref, gl_ref, af_ref, bf_ref, hb_ref,
                 ubuf, xq, qa, qb, carry_ref):
    i = pl.program_id(1)
    nc = pl.num_programs(1)
    ti = nc - 1 - i
    tc, d = x_ref.shape
    ncol = d // LANES
    sub = tc // SUBLANES
    gain = g_ref[...]

    hc = _rms(x_ref[...], gain)
    hcb = hc.astype(BF16)

    hall = jnp.concatenate([_rms(xp_ref[...], gain), hc, _rms(xn_ref[...], gain)], axis=0).astype(BF16)
    ul = jnp.dot(hall, _wb(w_ref[:, d:2 * d]), preferred_element_type=F32)
    for c in range(ncol):
        cs = slice(c * LANES, (c + 1) * LANES)
        ubuf[c, pl.ds(0, HALO), :] = jnp.where(ti > 0, ul[0:HALO, cs], 0.0)
        ubuf[c, pl.ds(HALO, tc), :] = ul[HALO:HALO + tc, cs]
        ubuf[c, pl.ds(HALO + tc, HALO), :] = jnp.where(ti < nc - 1, ul[HALO + tc:, cs], 0.0)

    wide = [(up_ref, 0, 0, d), (q_ref, 0, 2 * d, d), (gl_ref, 0, 3 * d, 3 * d)]
    chunk = 4 * LANES
    wide_chunks = [(ref, o + k, col + k) for ref, o, col, n in wide for k in range(0, n, chunk)]

    def wide_dot(ref, o, col):
        r = jnp.dot(hcb, _wb(w_ref[:, col:col + chunk]), preferred_element_type=F32)
        ref[:, o:o + chunk] = r.astype(ref.dtype)

    assert len(wide_chunks) >= ncol
    for c in range(ncol):
        cs = pl.ds(c * LANES, LANES)
        w = convw_ref[:, cs]
        xc = convb_ref[:, cs] + _shifted(ubuf, c, HALO - CONV_LEFT, tc) * w[0:1]
        for k in range(1, CONV_WIDTH):
            xc = xc + _shifted(ubuf, c, HALO - CONV_LEFT + k, tc) * w[k:k + 1]
        for j in range(SUBLANES):
            xq[c, pl.ds(j, sub, stride=SUBLANES), :] = xc[j * sub:(j + 1) * sub]
        wide_dot(*wide_chunks[c])

        xb = xq[c]
        g4 = jnp.dot(xb.astype(BF16), _wb(wg_ref[c]), preferred_element_type=F32)
        for dirn in range(2):
            o = 2 * dirn * LANES
            tr = jnp.tanh(g4[:, o:o + LANES] + bias_ref[2 * dirn:2 * dirn + 1, cs])
            tx = jnp.tanh(g4[:, o + LANES:o + 2 * LANES] + bias_ref[2 * dirn + 1:2 * dirn + 2, cs])
            nk = (-0.5 * LRU_C) * jax.nn.softplus(-lam_ref[dirn:dirn + 1, cs])
            log_a = nk * tr + nk
            a = jnp.exp(log_a)
            m2 = jnp.tanh(log_a) * (-1.0 - a * a)
            mult = m2 * lax.rsqrt(jnp.maximum(m2, 1e-30))
            b = mult * (0.5 * tx + 0.5) * xb
            if dirn == 0:
                af_ref[:, cs] = a
                bf_ref[:, cs] = b
            else:
                qa[:, cs] = a
                qb[:, cs] = b

    def emit(rows, h):
        hb_ref[rows, :] = h

    rest = wide_chunks[ncol:]
    for args in rest[:len(rest) // 2]:
        wide_dot(*args)
    _scan(True, i == 0, qa, qb, carry_ref, emit)
    for args in rest[len(rest) // 2:]:
        wide_dot(*args)


def _inproj(x, gain, w_in, convw, convb, wg, bias, lam):
    b, s_len, d = x.shape
    tc = min(TIME_CHUNK, s_len)
    nc = s_len // tc
    ncol = d // LANES
    rev = lambda i: nc - 1 - i
    out = lambda w: pl.BlockSpec((None, tc, w), lambda bb, i: (bb, rev(i), 0))
    f32_out = lambda w: jax.ShapeDtypeStruct((b, s_len, w), F32)
    return pl.pallas_call(
        _inproj_body,
        grid=(b, nc),
        in_specs=_halo_specs(tc, d, s_len, rev) + [
            _const_spec(gain.shape), _resident_spec(w_in.shape), _const_spec(convw.shape),
            _const_spec(convb.shape), _const_spec(wg.shape), _const_spec(bias.shape), _const_spec(lam.shape)],
        out_specs=[out(d), out(d), out(3 * d), out(d), out(d), out(d)],
        out_shape=[f32_out(d), jax.ShapeDtypeStruct((b, s_len, d), BF16), f32_out(3 * d),
                   f32_out(d), f32_out(d), f32_out(d)],
        scratch_shapes=[pltpu.VMEM((ncol, tc + 2 * HALO, LANES), F32), pltpu.VMEM((ncol, tc, LANES), F32),
                        pltpu.VMEM((tc, d), F32), pltpu.VMEM((tc, d), F32), pltpu.VMEM((SUBLANES, d), F32)],
        compiler_params=pltpu.CompilerParams(dimension_semantics=("arbitrary", "arbitrary"),
                                             vmem_limit_bytes=VMEM_LIMIT),
        name="inproj",
    )(x, x, x, gain, w_in, convw, convb, wg, bias, lam)


def _memkv_body(scale, mem_ref, g_ref, wk_ref, wv_ref, kt_ref, v_ref):
    mn = _rms(mem_ref[...], g_ref[...]).astype(BF16)
    k = jnp.dot(mn, wk_ref[...], preferred_element_type=F32)
    kt_ref[...] = (k.T * scale).astype(BF16)
    v_ref[...] = jnp.dot(mn, wv_ref[...], preferred_element_type=F32).astype(BF16)


def _memkv(mem, gain, wk, wv):
    b, m, d = mem.shape
    scale = float((d // N_XHEADS) ** -0.5)
    return pl.pallas_call(
        functools.partial(_memkv_body, scale),
        grid=(b,),
        in_specs=[pl.BlockSpec((None, m, d), lambda i: (i, 0, 0)), _const_spec((1, d)),
                  _const_spec(wk.shape), _const_spec(wv.shape)],
        out_specs=[pl.BlockSpec((None, d, m), lambda i: (i, 0, 0)),
                   pl.BlockSpec((None, m, d), lambda i: (i, 0, 0))],
        out_shape=[jax.ShapeDtypeStruct((b, d, m), BF16), jax.ShapeDtypeStruct((b, m, d), BF16)],
        compiler_params=pltpu.CompilerParams(dimension_semantics=("arbitrary",),
                                             vmem_limit_bytes=VMEM_LIMIT),
        name="memkv",
    )(mem, gain, wk, wv)


def _fill_halo_buf(buf, cur_ref, prev_ref, next_ref, has_prev, has_next):
    tc = cur_ref.shape[0]
    ncol = cur_ref.shape[1] // LANES
    for c in range(ncol):
        cs = pl.ds(c * LANES, LANES)
        buf[c, pl.ds(0, HALO), :] = jnp.where(has_prev, prev_ref[:, cs], 0.0)
        buf[c, pl.ds(HALO, tc), :] = cur_ref[:, cs]
        buf[c, pl.ds(HALO + tc, HALO), :] = jnp.where(has_next, next_ref[:, cs], 0.0)
        buf[c, pl.ds(2 * HALO + tc, PAD_ROWS - 2 * HALO), :] = jnp.zeros((PAD_ROWS - 2 * HALO, LANES), F32)


def _unpermute(qh, col, tc):
    sub = tc // SUBLANES
    return jnp.concatenate(
        [qh[col, pl.ds(j, sub, stride=SUBLANES), :] for j in range(SUBLANES)], axis=0)


def _window_sums(pbuf, t1, t2, group, tc):
    outs = []
    for c in (2 * group, 2 * group + 1):
        if group == 0:
            outs.append(_shifted(pbuf, c, 7, tc) + pbuf[c, pl.ds(8, tc), :])
        elif group == 1:
            outs.append((_shifted(pbuf, c, 6, tc) + _shifted(pbuf, c, 7, tc))
                        + (pbuf[c, pl.ds(8, tc), :] + _shifted(pbuf, c, 9, tc)))
        else:
            n2 = tc + 24 if group == 3 else tc + 16
            t1[0, pl.ds(0, n2), :] = pbuf[c, pl.ds(0, n2), :] + _shifted(pbuf, c, 1, n2)
            n4 = n2 - 8
            t2[0, pl.ds(0, n4), :] = t1[0, pl.ds(0, n4), :] + _shifted(t1, 0, 2, n4)
            if group == 2:
                outs.append(_shifted(t2, 0, 4, tc) + t2[0, pl.ds(8, tc), :])
            else:
                n8 = n4 - 8
                t1[0, pl.ds(0, n8), :] = t2[0, pl.ds(0, n8), :] + _shifted(t2, 0, 4, n8)
                outs.append(t1[0, pl.ds(0, tc), :] + t1[0, pl.ds(8, tc), :])
    return outs


def _mixmlp_body(s_len, nc, x_ref, up_ref, upp_ref, upn_ref, af_ref, bf_ref, hb_ref, q_ref, gl_ref,
                 kt_ref, v_ref, poolw_ref, pscale_ref, wout_ref, gpost_ref, mpre_ref, mpost_ref,
                 w1_ref, w2_ref, o_ref,
                 pbuf, qh, carry_ref, t1, t2, mbuf, xmid, hbuf, a2buf, yacc):
    s = pl.program_id(0)
    last = pl.num_programs(0) - 3
    i = jnp.minimum(s, last) % nc
    tc, d = x_ref.shape
    ncol = d // LANES
    n_ff = w1_ref.shape[1] // d

    @pl.when(s == 0)
    def _():
        xmid[...] = jnp.zeros(xmid.shape, F32)
        hbuf[...] = jnp.zeros(hbuf.shape, BF16)
        yacc[...] = jnp.zeros(yacc.shape, F32)

    pw = 2 * LANES
    n_piece = d // pw

    def up_piece(f, n):
        def run():
            a = jnp.maximum(jnp.dot(hbuf[...], _wb(w1_ref[:, f * d + n * pw:f * d + (n + 1) * pw]),
                                    preferred_element_type=F32), 0.0)
            a2buf[f % 2, :, n * pw:(n + 1) * pw] = (a * a).astype(BF16)
        return run

    def down_piece(f, n):
        def run():
            part = jnp.dot(a2buf[f % 2], _wb(w2_ref[f * d // 2:(f + 1) * d // 2, n * pw:(n + 1) * pw]),
                           preferred_element_type=F32)
            if f == 0:
                yacc[:, n * pw:(n + 1) * pw] = part
            else:
                yacc[:, n * pw:(n + 1) * pw] += part
        return run

    queue = [("up", up_piece(0, n)) for n in range(n_piece)]
    for f in range(n_ff):
        for n in range(n_piece):
            if f + 1 < n_ff:
                queue.append(("up", up_piece(f + 1, n)))
            queue.append(("down", down_piece(f, n)))
    queue.reverse()

    def mlp_step(k=1):
        for _ in range(k):
            if queue:
                queue.pop()[1]()

    mlp_step()
    o_ref[...] = xmid[s % 2] + _rms(yacc[...], mpost_ref[...])

    mlp_step()
    _fill_halo_buf(pbuf, up_ref, upp_ref, upn_ref, i > 0, i < nc - 1)

    def emit(rows, h):
        y = h + hb_ref[rows, :]
        for c in range(ncol):
            qh[c, rows, :] = y[:, c * LANES:(c + 1) * LANES]

    _scan(False, i == 0, af_ref, bf_ref, carry_ref, emit, tick=mlp_step, tick_every=16)

    t_glob = i * tc + lax.broadcasted_iota(jnp.int32, (tc, LANES), 0)
    gw = d // N_POOL_GROUPS
    for g in range(N_POOL_GROUPS):
        cols = pl.ds(g * gw, gw)
        half = POOL_WINDOWS[g] // 2
        count = jnp.minimum(t_glob + half, s_len) - jnp.maximum(t_glob - half, 0)
        inv = 1.0 / count.astype(F32)
        sums = _window_sums(pbuf, t1, t2, g, tc)
        pooled = jnp.concatenate(
            [sums[k] * inv - pbuf[2 * g + k, pl.ds(HALO, tc), :] for k in range(2)], axis=1)
        mlp_step()
        yp = jnp.dot(pooled.astype(BF16), _wb(poolw_ref[g]), preferred_element_type=F32) * pscale_ref[:, cols]
        mlp_step()

        sc = jnp.dot(q_ref[:, cols], kt_ref[cols, :], preferred_element_type=F32)
        e = jnp.exp(sc - jnp.max(sc, axis=-1, keepdims=True))
        p = e * (1.0 / jnp.sum(e, axis=-1, keepdims=True))
        mlp_step()
        ym = jnp.dot(p.astype(BF16), v_ref[:, cols], preferred_element_type=F32)
        mlp_step()

        yl = jnp.concatenate([_unpermute(qh, 2 * g, tc), _unpermute(qh, 2 * g + 1, tc)], axis=1)

        merged = (yp + yl + ym) + jnp.tanh(gl_ref[:, cols]) * yp \
            + jnp.tanh(gl_ref[:, pl.ds(d + g * gw, gw)]) * yl \
            + jnp.tanh(gl_ref[:, pl.ds(2 * d + g * gw, gw)]) * ym
        mbuf[:, cols] = merged.astype(BF16)
        mlp_step()

    rest = len(queue)
    out = jnp.dot(mbuf[...], _wb(wout_ref[...]), preferred_element_type=F32)
    mlp_step(rest // 3)
    xnew = x_ref[...] + _rms(out, gpost_ref[...])
    xmid[s % 2] = xnew
    mlp_step(rest // 3)
    assert all(kind == "down" for kind, _ in queue)
    hbuf[...] = _rms(xnew, mpre_ref[...]).astype(BF16)
    mlp_step(len(queue))


def _mixmlp(x, up, af, bf, hb, q, gl, kt, v, poolw, pscale, wout, gpost, mpre, mpost, w1, w2):
    b, s_len, d = x.shape
    tc = min(TIME_CHUNK, s_len)
    nc = s_len // tc
    total = b * nc
    ncol = d // LANES
    per = tc // HALO
    last_halo = s_len // HALO - 1
    assert w1.shape[1] // d >= N_POOL_GROUPS

    def mix(s):
        c = jnp.minimum(s, total - 1)
        return c // nc, c % nc

    def cur(w):
        return pl.BlockSpec((None, tc, w), lambda s: (*mix(s), 0))

    prev = pl.BlockSpec((None, HALO, d), lambda s: (mix(s)[0], jnp.maximum(mix(s)[1] * per - 1, 0), 0))
    nxt = pl.BlockSpec((None, HALO, d), lambda s: (mix(s)[0], jnp.minimum((mix(s)[1] + 1) * per, last_halo), 0))
    per_batch = lambda a: pl.BlockSpec((None,) + a.shape[1:], lambda s: (mix(s)[0], 0, 0))

    def out_map(s):
        c = jnp.maximum(s - 2, 0)
        return c // nc, c % nc, 0

    scratch = [pltpu.VMEM((ncol, tc + PAD_ROWS, LANES), F32), pltpu.VMEM((ncol, tc, LANES), F32),
               pltpu.VMEM((SUBLANES, d), F32),
               pltpu.VMEM((1, tc + PAD_ROWS, LANES), F32), pltpu.VMEM((1, tc + PAD_ROWS, LANES), F32),
               pltpu.VMEM((tc, d), BF16), pltpu.VMEM((2, tc, d), F32),
               pltpu.VMEM((tc, d), BF16), pltpu.VMEM((2, tc, d), BF16), pltpu.VMEM((tc, d), F32)]
    return pl.pallas_call(
        functools.partial(_mixmlp_body, s_len, nc),
        grid=(total + 2,),
        in_specs=[cur(d), cur(d), prev, nxt, cur(d), cur(d), cur(d), cur(d), cur(3 * d),
                  per_batch(kt), per_batch(v),
                  _resident_spec(poolw.shape), _const_spec(pscale.shape), _resident_spec(wout.shape),
                  _const_spec(gpost.shape), _const_spec(mpre.shape), _const_spec(mpost.shape),
                  _resident_spec(w1.shape), _resident_spec(w2.shape)],
        out_specs=pl.BlockSpec((None, tc, d), out_map),
        out_shape=jax.ShapeDtypeStruct((b, s_len, d), F32),
        scratch_shapes=scratch,
        compiler_params=pltpu.CompilerParams(dimension_semantics=("arbitrary",),
                                             vmem_limit_bytes=VMEM_LIMIT),
        name="mixmlp",
    )(x, up, up, up, af, bf, hb, q, gl, kt, v, poolw, pscale, wout, gpost, mpre, mpost, w1, w2)


def _pack_rows(w):
    w = w.astype(BF16)
    k, n = w.shape[-2:]
    pairs = jnp.swapaxes(w.reshape(w.shape[:-2] + (k // 2, 2, n)), -1, -2)
    return lax.bitcast_convert_type(pairs, jnp.uint32)


def _wb(packed):
    return pltpu.bitcast(packed, BF16)


def _prep_layer(l, p):
    d = p["w_in"].shape[1]
    row = lambda a: a.reshape(1, -1)
    w_in = p["w_in"][l]
    w_in = _pack_rows(jnp.concatenate([w_in[:, :3 * d], 0.5 * w_in[:, 3 * d:]], axis=1))
    wg = _pack_rows(0.5 * jnp.concatenate([p["lru_wa"][l, 0], p["lru_wx"][l, 0],
                                           p["lru_wa"][l, 1], p["lru_wx"][l, 1]], axis=-1))
    bias = 0.5 * jnp.stack([p["lru_ba"][l, 0], p["lru_bx"][l, 0], p["lru_ba"][l, 1], p["lru_bx"][l, 1]])
    return dict(
        g_pre=row(p["norm_mix_pre"][l]), g_post=row(p["norm_mix_post"][l]), g_mem=row(p["norm_mem"][l]),
        w_in=w_in, pool_w=_pack_rows(p["pool_w"][l]), pool_scale=row(p["pool_scale"][l]),
        conv_w=p["conv_w"][l], conv_b=row(p["conv_b"][l]), wg=wg, bias=bias, lam=p["lru_lambda"][l],
        wk=p["w_kv"][l][:, :d].astype(BF16), wv=p["w_kv"][l][:, d:].astype(BF16),
        w_out=_pack_rows(0.5 * p["w_out"][l]),
        m_pre=row(p["norm_mlp_pre"][l]), m_post=row(p["norm_mlp_post"][l]),
        w1=_pack_rows(p["mlp_w1"][l]), w2=_pack_rows(p["mlp_w2"][l]),
    )


def _trunk(x, mem, layers):
    b, s_len, d = x.shape
    for lp in layers:
        up, q, gl, af, bf, hb = _inproj(x, lp["g_pre"], lp["w_in"], lp["conv_w"], lp["conv_b"],
                                        lp["wg"], lp["bias"], lp["lam"])
        kt, v = _memkv(mem, lp["g_mem"], lp["wk"], lp["wv"])
        x = _mixmlp(x, up, af, bf, hb, q, gl, kt, v, lp["pool_w"], lp["pool_scale"], lp["w_out"], lp["g_post"],
                    lp["m_pre"], lp["m_post"], lp["w1"], lp["w2"])
    return x


def kernel(x_prompt, x_sample, mem_prompt, mem_sample, norm_mix_pre, norm_mix_post, norm_mem, w_in, pool_w, pool_scale, conv_w, conv_b, lru_wa, lru_ba, lru_wx, lru_bx, lru_lambda, w_kv, w_out, norm_mlp_pre, norm_mlp_post, mlp_w1, mlp_w2):
    p = dict(norm_mix_pre=norm_mix_pre, norm_mix_post=norm_mix_post, norm_mem=norm_mem, w_in=w_in,
             pool_w=pool_w, pool_scale=pool_scale, conv_w=conv_w, conv_b=conv_b, lru_wa=lru_wa,
             lru_ba=lru_ba, lru_wx=lru_wx, lru_bx=lru_bx, lru_lambda=lru_lambda, w_kv=w_kv, w_out=w_out,
             norm_mlp_pre=norm_mlp_pre, norm_mlp_post=norm_mlp_post, mlp_w1=mlp_w1, mlp_w2=mlp_w2)
    layers = [_prep_layer(l, p) for l in range(w_in.shape[0])]
    return (_trunk(x_prompt, mem_prompt, layers), _trunk(x_sample, mem_sample, layers))
```

```python
import functools

import jax
import jax.numpy as jnp
from jax import lax
from jax.experimental import pallas as pl
from jax.experimental.pallas import tpu as pltpu

F32 = jnp.float32
BF16 = jnp.bfloat16

LANES = 128
SUBLANES = 8
EPS = 1e-6
LRU_C = 8.0
POOL_WINDOWS = (2, 4, 8, 16)
N_POOL_GROUPS = len(POOL_WINDOWS)
N_XHEADS = 4
CONV_WIDTH = 4
CONV_LEFT = 2
HALO = 8
PAD_ROWS = 32
VMEM_LIMIT = 56 * 1024 * 1024

TIME_CHUNK = 256


def _rms(x, gain):
    ms = jnp.mean(x * x, axis=-1, keepdims=True)
    return x * lax.rsqrt(ms + EPS) * gain


def _const_spec(shape):
    nd = len(shape)
    return pl.BlockSpec(shape, lambda *_: (0,) * nd)


def _resident_spec(shape):
    nd = len(shape)
    return pl.BlockSpec(shape, lambda *_: (0,) * nd, pipeline_mode=pl.Buffered(1))


def _shifted(buf, col, start, rows):
    return buf[pl.ds(col, 1, stride=2), pl.ds(start, rows), :].reshape(rows, LANES)


def _halo_specs(tc, d, s_len, chunk_of):
    per = tc // HALO
    last = s_len // HALO - 1
    cur = pl.BlockSpec((None, tc, d), lambda b, i: (b, chunk_of(i), 0))
    prev = pl.BlockSpec((None, HALO, d), lambda b, i: (b, jnp.maximum(chunk_of(i) * per - 1, 0), 0))
    nxt = pl.BlockSpec((None, HALO, d), lambda b, i: (b, jnp.minimum((chunk_of(i) + 1) * per, last), 0))
    return [cur, prev, nxt]


def _scan(reverse, first_step, a_ref, b_ref, carry_ref, emit, tick=lambda: None, tick_every=8):
    tc, d = a_ref.shape
    sub = tc // SUBLANES

    @pl.when(first_step)
    def _():
        carry_ref[...] = jnp.zeros(carry_ref.shape, F32)

    def step(s):
        k = (sub - 1 - s) if reverse else s
        return pl.ds(k * SUBLANES, SUBLANES)

    shape = (SUBLANES, d)
    f, pf = jnp.zeros(shape, F32), jnp.ones(shape, F32)
    for s in range(sub):
        rows = step(s)
        a = a_ref[rows, :]
        f, pf = a * f + b_ref[rows, :], a * pf
        if (s + 1) % tick_every == 0:
            tick()

    row = lax.broadcasted_iota(jnp.int32, shape, 0)
    edge = (SUBLANES - 1) if reverse else 0
    shift = (SUBLANES - 1) if reverse else 1
    carry = carry_ref[...]
    init = carry
    for _ in range(SUBLANES - 1):
        init = jnp.where(row == edge, carry, pltpu.roll(f + pf * init, shift, 0))
    last = f + pf * init
    out_row = 0 if reverse else SUBLANES - 1
    carry_ref[...] = jnp.broadcast_to(last[out_row:out_row + 1, :], shape)
    tick()

    h = init
    for s in range(sub):
        rows = step(s)
        h = a_ref[rows, :] * h + b_ref[rows, :]
        emit(rows, h)
        if (s + 1) % tick_every == 0:
            tick()


def _inproj_body(x_ref, xp_ref, xn_ref, g_ref, w_ref, convw_ref, convb_ref, wg_ref, bias_ref, lam_ref,
                 up_ref, q_ref, gl_ref, af_ref, bf_ref, hb_ref,
                 ubuf, xq, qa, qb, carry_ref):
    i = pl.program_id(1)
    nc = pl.num_programs(1)
    ti = nc - 1 - i
    tc, d = x_ref.shape
    ncol = d // LANES
    sub = tc // SUBLANES
    gain = g_ref[...]

    hc = _rms(x_ref[...], gain)
    hcb = hc.astype(BF16)

    hall = jnp.concatenate([_rms(xp_ref[...], gain), hc, _rms(xn_ref[...], gain)], axis=0).astype(BF16)
    ul = jnp.dot(hall, _wb(w_ref[:, d:2 * d]), preferred_element_type=F32)
    for c in range(ncol):
        cs = slice(c * LANES, (c + 1) * LANES)
        ubuf[c, pl.ds(0, HALO), :] = jnp.where(ti > 0, ul[0:HALO, cs], 0.0)
        ubuf[c, pl.ds(HALO, tc), :] = ul[HALO:HALO + tc, cs]
        ubuf[c, pl.ds(HALO + tc, HALO), :] = jnp.where(ti < nc - 1, ul[HALO + tc:, cs], 0.0)

    wide = [(up_ref, 0, 0, d), (q_ref, 0, 2 * d, d), (gl_ref, 0, 3 * d, 3 * d)]
    chunk = 4 * LANES
    wide_chunks = [(ref, o + k, col + k) for ref, o, col, n in wide for k in range(0, n, chunk)]

    def wide_dot(ref, o, col):
        r = jnp.dot(hcb, _wb(w_ref[:, col:col + chunk]), preferred_element_type=F32)
        ref[:, o:o + chunk] = r.astype(ref.dtype)

    assert len(wide_chunks) >= ncol
    for c in range(ncol):
        cs = pl.ds(c * LANES, LANES)
        w = convw_ref[:, cs]
        xc = convb_ref[:, cs] + _shifted(ubuf, c, HALO - CONV_LEFT, tc) * w[0:1]
        for k in range(1, CONV_WIDTH):
            xc = xc + _shifted(ubuf, c, HALO - CONV_LEFT + k, tc) * w[k:k + 1]
        for j in range(SUBLANES):
            xq[c, pl.ds(j, sub, stride=SUBLANES), :] = xc[j * sub:(j + 1) * sub]
        wide_dot(*wide_chunks[c])

        xb = xq[c]
        g4 = jnp.dot(xb.astype(BF16), _wb(wg_ref[c]), preferred_element_type=F32)
        for dirn in range(2):
            o = 2 * dirn * LANES
            tr = jnp.tanh(g4[:, o:o + LANES] + bias_ref[2 * dirn:2 * dirn + 1, cs])
            tx = jnp.tanh(g4[:, o + LANES:o + 2 * LANES] + bias_ref[2 * dirn + 1:2 * dirn + 2, cs])
            nk = (-0.5 * LRU_C) * jax.nn.softplus(-lam_ref[dirn:dirn + 1, cs])
            log_a = nk * tr + nk
            a = jnp.exp(log_a)
            m2 = jnp.tanh(log_a) * (-1.0 - a * a)
            mult = m2 * lax.rsqrt(jnp.maximum(m2, 1e-30))
            b = mult * (0.5 * tx + 0.5) * xb
            if dirn == 0:
                af_ref[:, cs] = a
                bf_ref[:, cs] = b
            else:
                qa[:, cs] = a
                qb[:, cs] = b

    def emit(rows, h):
        hb_ref[rows, :] = h

    rest = wide_chunks[ncol:]
    for args in rest[:len(rest) // 2]:
        wide_dot(*args)
    _scan(True, i == 0, qa, qb, carry_ref, emit)
    for args in rest[len(rest) // 2:]:
        wide_dot(*args)


def _inproj(x, gain, w_in, convw, convb, wg, bias, lam):
    b, s_len, d = x.shape
    tc = min(TIME_CHUNK, s_len)
    nc = s_len // tc
    ncol = d // LANES
    rev = lambda i: nc - 1 - i
    out = lambda w: pl.BlockSpec((None, tc, w), lambda bb, i: (bb, rev(i), 0))
    f32_out = lambda w: jax.ShapeDtypeStruct((b, s_len, w), F32)
    return pl.pallas_call(
        _inproj_body,
        grid=(b, nc),
        in_specs=_halo_specs(tc, d, s_len, rev) + [
            _const_spec(gain.shape), _resident_spec(w_in.shape), _const_spec(convw.shape),
            _const_spec(convb.shape), _const_spec(wg.shape), _const_spec(bias.shape), _const_spec(lam.shape)],
        out_specs=[out(d), out(d), out(3 * d), out(d), out(d), out(d)],
        out_shape=[f32_out(d), jax.ShapeDtypeStruct((b, s_len, d), BF16), f32_out(3 * d),
                   f32_out(d), f32_out(d), f32_out(d)],
        scratch_shapes=[pltpu.VMEM((ncol, tc + 2 * HALO, LANES), F32), pltpu.VMEM((ncol, tc, LANES), F32),
                        pltpu.VMEM((tc, d), F32), pltpu.VMEM((tc, d), F32), pltpu.VMEM((SUBLANES, d), F32)],
        compiler_params=pltpu.CompilerParams(dimension_semantics=("arbitrary", "arbitrary"),
                                             vmem_limit_bytes=VMEM_LIMIT),
        name="inproj",
    )(x, x, x, gain, w_in, convw, convb, wg, bias, lam)


def _memkv_body(scale, mem_ref, g_ref, wk_ref, wv_ref, kt_ref, v_ref):
    mn = _rms(mem_ref[...], g_ref[...]).astype(BF16)
    k = jnp.dot(mn, wk_ref[...], preferred_element_type=F32)
    kt_ref[...] = (k.T * scale).astype(BF16)
    v_ref[...] = jnp.dot(mn, wv_ref[...], preferred_element_type=F32).astype(BF16)


def _memkv(mem, gain, wk, wv):
    b, m, d = mem.shape
    scale = float((d // N_XHEADS) ** -0.5)
    return pl.pallas_call(
        functools.partial(_memkv_body, scale),
        grid=(b,),
        in_specs=[pl.BlockSpec((None, m, d), lambda i: (i, 0, 0)), _const_spec((1, d)),
                  _const_spec(wk.shape), _const_spec(wv.shape)],
        out_specs=[pl.BlockSpec((None, d, m), lambda i: (i, 0, 0)),
                   pl.BlockSpec((None, m, d), lambda i: (i, 0, 0))],
        out_shape=[jax.ShapeDtypeStruct((b, d, m), BF16), jax.ShapeDtypeStruct((b, m, d), BF16)],
        compiler_params=pltpu.CompilerParams(dimension_semantics=("arbitrary",),
                                             vmem_limit_bytes=VMEM_LIMIT),
        name="memkv",
    )(mem, gain, wk, wv)


def _fill_halo_buf(buf, cur_ref, prev_ref, next_ref, has_prev, has_next):
    tc = cur_ref.shape[0]
    ncol = cur_ref.shape[1] // LANES
    for c in range(ncol):
        cs = pl.ds(c * LANES, LANES)
        buf[c, pl.ds(0, HALO), :] = jnp.where(has_prev, prev_ref[:, cs], 0.0)
        buf[c, pl.ds(HALO, tc), :] = cur_ref[:, cs]
        buf[c, pl.ds(HALO + tc, HALO), :] = jnp.where(has_next, next_ref[:, cs], 0.0)
        buf[c, pl.ds(2 * HALO + tc, PAD_ROWS - 2 * HALO), :] = jnp.zeros((PAD_ROWS - 2 * HALO, LANES), F32)


def _unpermute(qh, col, tc):
    sub = tc // SUBLANES
    return jnp.concatenate(
        [qh[col, pl.ds(j, sub, stride=SUBLANES), :] for j in range(SUBLANES)], axis=0)


def _window_sums(pbuf, t1, t2, group, tc):
    outs = []
    for c in (2 * group, 2 * group + 1):
        if group == 0:
            outs.append(_shifted(pbuf, c, 7, tc) + pbuf[c, pl.ds(8, tc), :])
        elif group == 1:
            outs.append((_shifted(pbuf, c, 6, tc) + _shifted(pbuf, c, 7, tc))
                        + (pbuf[c, pl.ds(8, tc), :] + _shifted(pbuf, c, 9, tc)))
        else:
            n2 = tc + 24 if group == 3 else tc + 16
            t1[0, pl.ds(0, n2), :] = pbuf[c, pl.ds(0, n2), :] + _shifted(pbuf, c, 1, n2)
            n4 = n2 - 8
            t2[0, pl.ds(0, n4), :] = t1[0, pl.ds(0, n4), :] + _shifted(t1, 0, 2, n4)
            if group == 2:
                outs.append(_shifted(t2, 0, 4, tc) + t2[0, pl.ds(8, tc), :])
            else:
                n8 = n4 - 8
                t1[0, pl.ds(0, n8), :] = t2[0, pl.ds(0, n8), :] + _shifted(t2, 0, 4, n8)
                outs.append(t1[0, pl.ds(0, tc), :] + t1[0, pl.ds(8, tc), :])
    return outs


def _mixmlp_body(s_len, nc, x_ref, up_ref, upp_ref, upn_ref, af_ref, bf_ref, hb_ref, q_ref, gl_ref,
                 kt_ref, v_ref, poolw_ref, pscale_ref, wout_ref, gpost_ref, mpre_ref, mpost_ref,
                 w1_ref, w2_ref, o_ref,
                 pbuf, qh, carry_ref, t1, t2, mbuf, xmid, hbuf, a2buf, yacc):
    s = pl.program_id(0)
    last = pl.num_programs(0) - 3
    i = jnp.minimum(s, last) % nc
    tc, d = x_ref.shape
    ncol = d // LANES
    n_ff = w1_ref.shape[1] // d

    @pl.when(s == 0)
    def _():
        xmid[...] = jnp.zeros(xmid.shape, F32)
        hbuf[...] = jnp.zeros(hbuf.shape, BF16)
        yacc[...] = jnp.zeros(yacc.shape, F32)

    pw = 2 * LANES
    n_piece = d // pw

    def up_piece(f, n):
        def run():
            a = jnp.maximum(jnp.dot(hbuf[...], _wb(w1_ref[:, f * d + n * pw:f * d + (n + 1) * pw]),
                                    preferred_element_type=F32), 0.0)
            a2buf[f % 2, :, n * pw:(n + 1) * pw] = (a * a).astype(BF16)
        return run

    def down_piece(f, n):
        def run():
            part = jnp.dot(a2buf[f % 2], _wb(w2_ref[f * d // 2:(f + 1) * d // 2, n * pw:(n + 1) * pw]),
                           preferred_element_type=F32)
            if f == 0:
                yacc[:, n * pw:(n + 1) * pw] = part
            else:
                yacc[:, n * pw:(n + 1) * pw] += part
        return run

    queue = [("up", up_piece(0, n)) for n in range(n_piece)]
    for f in range(n_ff):
        for n in range(n_piece):
            if f + 1 < n_ff:
                queue.append(("up", up_piece(f + 1, n)))
            queue.append(("down", down_piece(f, n)))
    queue.reverse()

    def mlp_step(k=1):
        for _ in range(k):
            if queue:
                queue.pop()[1]()

    mlp_step()
    o_ref[...] = xmid[s % 2] + _rms(yacc[...], mpost_ref[...])

    mlp_step()
    _fill_halo_buf(pbuf, up_ref, upp_ref, upn_ref, i > 0, i < nc - 1)

    def emit(rows, h):
        y = h + hb_ref[rows, :]
        for c in range(ncol):
            qh[c, rows, :] = y[:, c * LANES:(c + 1) * LANES]

    _scan(False, i == 0, af_ref, bf_ref, carry_ref, emit, tick=mlp_step, tick_every=16)

    t_glob = i * tc + lax.broadcasted_iota(jnp.int32, (tc, LANES), 0)
    gw = d // N_POOL_GROUPS
    for g in range(N_POOL_GROUPS):
        cols = pl.ds(g * gw, gw)
        half = POOL_WINDOWS[g] // 2
        count = jnp.minimum(t_glob + half, s_len) - jnp.maximum(t_glob - half, 0)
        inv = 1.0 / count.astype(F32)
        sums = _window_sums(pbuf, t1, t2, g, tc)
        pooled = jnp.concatenate(
            [sums[k] * inv - pbuf[2 * g + k, pl.ds(HALO, tc), :] for k in range(2)], axis=1)
        mlp_step()
        yp = jnp.dot(pooled.astype(BF16), _wb(poolw_ref[g]), preferred_element_type=F32) * pscale_ref[:, cols]
        mlp_step()

        sc = jnp.dot(q_ref[:, cols], kt_ref[cols, :], preferred_element_type=F32)
        e = jnp.exp(sc - jnp.max(sc, axis=-1, keepdims=True))
        p = e * (1.0 / jnp.sum(e, axis=-1, keepdims=True))
        mlp_step()
        ym = jnp.dot(p.astype(BF16), v_ref[:, cols], preferred_element_type=F32)
        mlp_step()

        yl = jnp.concatenate([_unpermute(qh, 2 * g, tc), _unpermute(qh, 2 * g + 1, tc)], axis=1)

        merged = (yp + yl + ym) + jnp.tanh(gl_ref[:, cols]) * yp \
            + jnp.tanh(gl_ref[:, pl.ds(d + g * gw, gw)]) * yl \
            + jnp.tanh(gl_ref[:, pl.ds(2 * d + g * gw, gw)]) * ym
        mbuf[:, cols] = merged.astype(BF16)
        mlp_step()

    rest = len(queue)
    out = jnp.dot(mbuf[...], _wb(wout_ref[...]), preferred_element_type=F32)
    mlp_step(rest // 3)
    xnew = x_ref[...] + _rms(out, gpost_ref[...])
    xmid[s % 2] = xnew
    mlp_step(rest // 3)
    assert all(kind == "down" for kind, _ in queue)
    hbuf[...] = _rms(xnew, mpre_ref[...]).astype(BF16)
    mlp_step(len(queue))


def _mixmlp(x, up, af, bf, hb, q, gl, kt, v, poolw, pscale, wout, gpost, mpre, mpost, w1, w2):
    b, s_len, d = x.shape
    tc = min(TIME_CHUNK, s_len)
    nc = s_len // tc
    total = b * nc
    ncol = d // LANES
    per = tc // HALO
    last_halo = s_len // HALO - 1
    assert w1.shape[1] // d >= N_POOL_GROUPS

    def mix(s):
        c = jnp.minimum(s, total - 1)
        return c // nc, c % nc

    def cur(w):
        return pl.BlockSpec((None, tc, w), lambda s: (*mix(s), 0))

    prev = pl.BlockSpec((None, HALO, d), lambda s: (mix(s)[0], jnp.maximum(mix(s)[1] * per - 1, 0), 0))
    nxt = pl.BlockSpec((None, HALO, d), lambda s: (mix(s)[0], jnp.minimum((mix(s)[1] + 1) * per, last_halo), 0))
    per_batch = lambda a: pl.BlockSpec((None,) + a.shape[1:], lambda s: (mix(s)[0], 0, 0))

    def out_map(s):
        c = jnp.maximum(s - 2, 0)
        return c // nc, c % nc, 0

    scratch = [pltpu.VMEM((ncol, tc + PAD_ROWS, LANES), F32), pltpu.VMEM((ncol, tc, LANES), F32),
               pltpu.VMEM((SUBLANES, d), F32),
               pltpu.VMEM((1, tc + PAD_ROWS, LANES), F32), pltpu.VMEM((1, tc + PAD_ROWS, LANES), F32),
               pltpu.VMEM((tc, d), BF16), pltpu.VMEM((2, tc, d), F32),
               pltpu.VMEM((tc, d), BF16), pltpu.VMEM((2, tc, d), BF16), pltpu.VMEM((tc, d), F32)]
    return pl.pallas_call(
        functools.partial(_mixmlp_body, s_len, nc),
        grid=(total + 2,),
        in_specs=[cur(d), cur(d), prev, nxt, cur(d), cur(d), cur(d), cur(d), cur(3 * d),
                  per_batch(kt), per_batch(v),
                  _resident_spec(poolw.shape), _const_spec(pscale.shape), _resident_spec(wout.shape),
                  _const_spec(gpost.shape), _const_spec(mpre.shape), _const_spec(mpost.shape),
                  _resident_spec(w1.shape), _resident_spec(w2.shape)],
        out_specs=pl.BlockSpec((None, tc, d), out_map),
        out_shape=jax.ShapeDtypeStruct((b, s_len, d), F32),
        scratch_shapes=scratch,
        compiler_params=pltpu.CompilerParams(dimension_semantics=("arbitrary",),
                                             vmem_limit_bytes=VMEM_LIMIT),
        name="mixmlp",
    )(x, up, up, up, af, bf, hb, q, gl, kt, v, poolw, pscale, wout, gpost, mpre, mpost, w1, w2)


def _pack_rows(w):
    bits = lambda a: lax.bitcast_convert_type(a.astype(BF16), jnp.uint16).astype(jnp.uint32)
    return bits(w[..., 0::2, :]) | (bits(w[..., 1::2, :]) << 16)


def _wb(packed):
    return pltpu.bitcast(packed, BF16)


def _prep_layer(l, p):
    d = p["w_in"].shape[1]
    row = lambda a: a.reshape(1, -1)
    w_in = p["w_in"][l]
    w_in = _pack_rows(jnp.concatenate([w_in[:, :3 * d], 0.5 * w_in[:, 3 * d:]], axis=1))
    wg = _pack_rows(0.5 * jnp.concatenate([p["lru_wa"][l, 0], p["lru_wx"][l, 0],
                                           p["lru_wa"][l, 1], p["lru_wx"][l, 1]], axis=-1))
    bias = 0.5 * jnp.stack([p["lru_ba"][l, 0], p["lru_bx"][l, 0], p["lru_ba"][l, 1], p["lru_bx"][l, 1]])
    return dict(
        g_pre=row(p["norm_mix_pre"][l]), g_post=row(p["norm_mix_post"][l]), g_mem=row(p["norm_mem"][l]),
        w_in=w_in, pool_w=_pack_rows(p["pool_w"][l]), pool_scale=row(p["pool_scale"][l]),
        conv_w=p["conv_w"][l], conv_b=row(p["conv_b"][l]), wg=wg, bias=bias, lam=p["lru_lambda"][l],
        wk=p["w_kv"][l][:, :d].astype(BF16), wv=p["w_kv"][l][:, d:].astype(BF16),
        w_out=_pack_rows(0.5 * p["w_out"][l]),
        m_pre=row(p["norm_mlp_pre"][l]), m_post=row(p["norm_mlp_post"][l]),
        w1=_pack_rows(p["mlp_w1"][l]), w2=_pack_rows(p["mlp_w2"][l]),
    )


def _trunk(x, mem, layers):
    b, s_len, d = x.shape
    for lp in layers:
        up, q, gl, af, bf, hb = _inproj(x, lp["g_pre"], lp["w_in"], lp["conv_w"], lp["conv_b"],
                                        lp["wg"], lp["bias"], lp["lam"])
        kt, v = _memkv(mem, lp["g_mem"], lp["wk"], lp["wv"])
        x = _mixmlp(x, up, af, bf, hb, q, gl, kt, v, lp["pool_w"], lp["pool_scale"], lp["w_out"], lp["g_post"],
                    lp["m_pre"], lp["m_post"], lp["w1"], lp["w2"])
    return x


def kernel(x_prompt, x_sample, mem_prompt, mem_sample, norm_mix_pre, norm_mix_post, norm_mem, w_in, pool_w, pool_scale, conv_w, conv_b, lru_wa, lru_ba, lru_wx, lru_bx, lru_lambda, w_kv, w_out, norm_mlp_pre, norm_mlp_post, mlp_w1, mlp_w2):
    p = dict(norm_mix_pre=norm_mix_pre, norm_mix_post=norm_mix_post, norm_mem=norm_mem, w_in=w_in,
             pool_w=pool_w, pool_scale=pool_scale, conv_w=conv_w, conv_b=conv_b, lru_wa=lru_wa,
             lru_ba=lru_ba, lru_wx=lru_wx, lru_bx=lru_bx, lru_lambda=lru_lambda, w_kv=w_kv, w_out=w_out,
             norm_mlp_pre=norm_mlp_pre, norm_mlp_post=norm_mlp_post, mlp_w1=mlp_w1, mlp_w2=mlp_w2)
    layers = [_prep_layer(l, p) for l in range(w_in.shape[0])]
    return (_trunk(x_prompt, mem_prompt, layers), _trunk(x_sample, mem_sample, layers))
```

```python
import functools

import jax
import jax.numpy as jnp
from jax import lax
from jax.experimental import pallas as pl
from jax.experimental.pallas import tpu as pltpu

F32 = jnp.float32
BF16 = jnp.bfloat16

LANES = 128
SUBLANES = 8
EPS = 1e-6
LRU_C = 8.0
POOL_WINDOWS = (2, 4, 8, 16)
N_POOL_GROUPS = len(POOL_WINDOWS)
N_XHEADS = 4
CONV_WIDTH = 4
CONV_LEFT = 2
HALO = 8
PAD_ROWS = 32
VMEM_LIMIT = 56 * 1024 * 1024

TIME_CHUNK = 256


def _rms(x, gain):
    ms = jnp.mean(x * x, axis=-1, keepdims=True)
    return x * lax.rsqrt(ms + EPS) * gain


def _const_spec(shape):
    nd = len(shape)
    return pl.BlockSpec(shape, lambda *_: (0,) * nd)


def _resident_spec(shape):
    nd = len(shape)
    return pl.BlockSpec(shape, lambda *_: (0,) * nd, pipeline_mode=pl.Buffered(1))


def _shifted(buf, col, start, rows):
    return buf[pl.ds(col, 1, stride=2), pl.ds(start, rows), :].reshape(rows, LANES)


def _halo_specs(tc, d, s_len, chunk_of):
    per = tc // HALO
    last = s_len // HALO - 1
    cur = pl.BlockSpec((None, tc, d), lambda b, i: (b, chunk_of(i), 0))
    prev = pl.BlockSpec((None, HALO, d), lambda b, i: (b, jnp.maximum(chunk_of(i) * per - 1, 0), 0))
    nxt = pl.BlockSpec((None, HALO, d), lambda b, i: (b, jnp.minimum((chunk_of(i) + 1) * per, last), 0))
    return [cur, prev, nxt]


def _scan(reverse, first_step, a_ref, b_ref, carry_ref, emit, tick=lambda: None, tick_every=8):
    tc, d = a_ref.shape
    sub = tc // SUBLANES

    @pl.when(first_step)
    def _():
        carry_ref[...] = jnp.zeros(carry_ref.shape, F32)

    def step(s):
        k = (sub - 1 - s) if reverse else s
        return pl.ds(k * SUBLANES, SUBLANES)

    shape = (SUBLANES, d)
    f, pf = jnp.zeros(shape, F32), jnp.ones(shape, F32)
    for s in range(sub):
        rows = step(s)
        a = a_ref[rows, :]
        f, pf = a * f + b_ref[rows, :], a * pf
        if (s + 1) % tick_every == 0:
            tick()

    row = lax.broadcasted_iota(jnp.int32, shape, 0)
    edge = (SUBLANES - 1) if reverse else 0
    shift = (SUBLANES - 1) if reverse else 1
    carry = carry_ref[...]
    init = carry
    for _ in range(SUBLANES - 1):
        init = jnp.where(row == edge, carry, pltpu.roll(f + pf * init, shift, 0))
    last = f + pf * init
    out_row = 0 if reverse else SUBLANES - 1
    carry_ref[...] = jnp.broadcast_to(last[out_row:out_row + 1, :], shape)
    tick()

    h = init
    for s in range(sub):
        rows = step(s)
        h = a_ref[rows, :] * h + b_ref[rows, :]
        emit(rows, h)
        if (s + 1) % tick_every == 0:
            tick()


def _inproj_body(x_ref, xp_ref, xn_ref, g_ref, w_ref, convw_ref, convb_ref, wg_ref, bias_ref, lam_ref,
                 up_ref, q_ref, gl_ref, af_ref, bf_ref, hb_ref,
                 ubuf, xq, qa, qb, carry_ref):
    i = pl.program_id(1)
    nc = pl.num_programs(1)
    ti = nc - 1 - i
    tc, d = x_ref.shape
    ncol = d // LANES
    sub = tc // SUBLANES
    gain = g_ref[...]

    hc = _rms(x_ref[...], gain)
    hcb = hc.astype(BF16)

    hall = jnp.concatenate([_rms(xp_ref[...], gain), hc, _rms(xn_ref[...], gain)], axis=0).astype(BF16)
    ul = jnp.dot(hall, _wb(w_ref[:, d:2 * d]), preferred_element_type=F32)
    for c in range(ncol):
        cs = slice(c * LANES, (c + 1) * LANES)
        ubuf[c, pl.ds(0, HALO), :] = jnp.where(ti > 0, ul[0:HALO, cs], 0.0)
        ubuf[c, pl.ds(HALO, tc), :] = ul[HALO:HALO + tc, cs]
        ubuf[c, pl.ds(HALO + tc, HALO), :] = jnp.where(ti < nc - 1, ul[HALO + tc:, cs], 0.0)

    wide = [(up_ref, 0, 0, d), (q_ref, 0, 2 * d, d), (gl_ref, 0, 3 * d, 3 * d)]
    chunk = 4 * LANES
    wide_chunks = [(ref, o + k, col + k) for ref, o, col, n in wide for k in range(0, n, chunk)]

    def wide_dot(ref, o, col):
        r = jnp.dot(hcb, _wb(w_ref[:, col:col + chunk]), preferred_element_type=F32)
        ref[:, o:o + chunk] = r.astype(ref.dtype)

    assert len(wide_chunks) >= ncol
    for c in range(ncol):
        cs = pl.ds(c * LANES, LANES)
        w = convw_ref[:, cs]
        xc = convb_ref[:, cs] + _shifted(ubuf, c, HALO - CONV_LEFT, tc) * w[0:1]
        for k in range(1, CONV_WIDTH):
            xc = xc + _shifted(ubuf, c, HALO - CONV_LEFT + k, tc) * w[k:k + 1]
        for j in range(SUBLANES):
            xq[c, pl.ds(j, sub, stride=SUBLANES), :] = xc[j * sub:(j + 1) * sub]
        wide_dot(*wide_chunks[c])

        xb = xq[c]
        g4 = jnp.dot(xb.astype(BF16), _wb(wg_ref[c]), preferred_element_type=F32)
        for dirn in range(2):
            o = 2 * dirn * LANES
            tr = jnp.tanh(g4[:, o:o + LANES] + bias_ref[2 * dirn:2 * dirn + 1, cs])
            tx = jnp.tanh(g4[:, o + LANES:o + 2 * LANES] + bias_ref[2 * dirn + 1:2 * dirn + 2, cs])
            nk = (-0.5 * LRU_C) * jax.nn.softplus(-lam_ref[dirn:dirn + 1, cs])
            log_a = nk * tr + nk
            a = jnp.exp(log_a)
            m2 = jnp.tanh(log_a) * (-1.0 - a * a)
            mult = m2 * lax.rsqrt(jnp.maximum(m2, 1e-30))
            b = mult * (0.5 * tx + 0.5) * xb
            if dirn == 0:
                af_ref[:, cs] = a
                bf_ref[:, cs] = b
            else:
                qa[:, cs] = a
                qb[:, cs] = b

    def emit(rows, h):
        hb_ref[rows, :] = h

    rest = wide_chunks[ncol:]
    for args in rest[:len(rest) // 2]:
        wide_dot(*args)
    _scan(True, i == 0, qa, qb, carry_ref, emit)
    for args in rest[len(rest) // 2:]:
        wide_dot(*args)


def _inproj(x, gain, w_in, convw, convb, wg, bias, lam):
    b, s_len, d = x.shape
    tc = min(TIME_CHUNK, s_len)
    nc = s_len // tc
    ncol = d // LANES
    rev = lambda i: nc - 1 - i
    out = lambda w: pl.BlockSpec((None, tc, w), lambda bb, i: (bb, rev(i), 0))
    f32_out = lambda w: jax.ShapeDtypeStruct((b, s_len, w), F32)
    return pl.pallas_call(
        _inproj_body,
        grid=(b, nc),
        in_specs=_halo_specs(tc, d, s_len, rev) + [
            _const_spec(gain.shape), _resident_spec(w_in.shape), _const_spec(convw.shape),
            _const_spec(convb.shape), _const_spec(wg.shape), _const_spec(bias.shape), _const_spec(lam.shape)],
        out_specs=[out(d), out(d), out(3 * d), out(d), out(d), out(d)],
        out_shape=[f32_out(d), jax.ShapeDtypeStruct((b, s_len, d), BF16), f32_out(3 * d),
                   f32_out(d), f32_out(d), f32_out(d)],
        scratch_shapes=[pltpu.VMEM((ncol, tc + 2 * HALO, LANES), F32), pltpu.VMEM((ncol, tc, LANES), F32),
                        pltpu.VMEM((tc, d), F32), pltpu.VMEM((tc, d), F32), pltpu.VMEM((SUBLANES, d), F32)],
        compiler_params=pltpu.CompilerParams(dimension_semantics=("arbitrary", "arbitrary"),
                                             vmem_limit_bytes=VMEM_LIMIT),
        name="inproj",
    )(x, x, x, gain, w_in, convw, convb, wg, bias, lam)


def _memkv_body(scale, mem_ref, g_ref, wk_ref, wv_ref, kt_ref, v_ref):
    mn = _rms(mem_ref[...], g_ref[...]).astype(BF16)
    k = jnp.dot(mn, wk_ref[...], preferred_element_type=F32)
    kt_ref[...] = (k.T * scale).astype(BF16)
    v_ref[...] = jnp.dot(mn, wv_ref[...], preferred_element_type=F32).astype(BF16)


def _memkv(mem, gain, wk, wv):
    b, m, d = mem.shape
    scale = float((d // N_XHEADS) ** -0.5)
    return pl.pallas_call(
        functools.partial(_memkv_body, scale),
        grid=(b,),
        in_specs=[pl.BlockSpec((None, m, d), lambda i: (i, 0, 0)), _const_spec((1, d)),
                  _const_spec(wk.shape), _const_spec(wv.shape)],
        out_specs=[pl.BlockSpec((None, d, m), lambda i: (i, 0, 0)),
                   pl.BlockSpec((None, m, d), lambda i: (i, 0, 0))],
        out_shape=[jax.ShapeDtypeStruct((b, d, m), BF16), jax.ShapeDtypeStruct((b, m, d), BF16)],
        compiler_params=pltpu.CompilerParams(dimension_semantics=("arbitrary",),
                                             vmem_limit_bytes=VMEM_LIMIT),
        name="memkv",
    )(mem, gain, wk, wv)


def _fill_halo_buf(buf, cur_ref, prev_ref, next_ref, has_prev, has_next):
    tc = cur_ref.shape[0]
    ncol = cur_ref.shape[1] // LANES
    for c in range(ncol):
        cs = pl.ds(c * LANES, LANES)
        buf[c, pl.ds(0, HALO), :] = jnp.where(has_prev, prev_ref[:, cs], 0.0)
        buf[c, pl.ds(HALO, tc), :] = cur_ref[:, cs]
        buf[c, pl.ds(HALO + tc, HALO), :] = jnp.where(has_next, next_ref[:, cs], 0.0)
        buf[c, pl.ds(2 * HALO + tc, PAD_ROWS - 2 * HALO), :] = jnp.zeros((PAD_ROWS - 2 * HALO, LANES), F32)


def _unpermute(qh, col, tc):
    sub = tc // SUBLANES
    return jnp.concatenate(
        [qh[col, pl.ds(j, sub, stride=SUBLANES), :] for j in range(SUBLANES)], axis=0)


def _window_sums(pbuf, t1, t2, group, tc):
    outs = []
    for c in (2 * group, 2 * group + 1):
        if group == 0:
            outs.append(_shifted(pbuf, c, 7, tc) + pbuf[c, pl.ds(8, tc), :])
        elif group == 1:
            outs.append((_shifted(pbuf, c, 6, tc) + _shifted(pbuf, c, 7, tc))
                        + (pbuf[c, pl.ds(8, tc), :] + _shifted(pbuf, c, 9, tc)))
        else:
            n2 = tc + 24 if group == 3 else tc + 16
            t1[0, pl.ds(0, n2), :] = pbuf[c, pl.ds(0, n2), :] + _shifted(pbuf, c, 1, n2)
            n4 = n2 - 8
            t2[0, pl.ds(0, n4), :] = t1[0, pl.ds(0, n4), :] + _shifted(t1, 0, 2, n4)
            if group == 2:
                outs.append(_shifted(t2, 0, 4, tc) + t2[0, pl.ds(8, tc), :])
            else:
                n8 = n4 - 8
                t1[0, pl.ds(0, n8), :] = t2[0, pl.ds(0, n8), :] + _shifted(t2, 0, 4, n8)
                outs.append(t1[0, pl.ds(0, tc), :] + t1[0, pl.ds(8, tc), :])
    return outs


def _mixmlp_body(s_len, nc, x_ref, up_ref, upp_ref, upn_ref, af_ref, bf_ref, hb_ref, q_ref, gl_ref,
                 kt_ref, v_ref, poolw_ref, pscale_ref, wout_ref, gpost_ref, mpre_ref, mpost_ref,
                 w1_ref, w2_ref, o_ref,
                 pbuf, qh, carry_ref, t1, t2, mbuf, xmid, hbuf, a2buf, yacc):
    s = pl.program_id(0)
    last = pl.num_programs(0) - 3
    i = jnp.minimum(s, last) % nc
    tc, d = x_ref.shape
    ncol = d // LANES
    n_ff = w1_ref.shape[1] // d

    @pl.when(s == 0)
    def _():
        xmid[...] = jnp.zeros(xmid.shape, F32)
        hbuf[...] = jnp.zeros(hbuf.shape, BF16)
        yacc[...] = jnp.zeros(yacc.shape, F32)

    pw = 2 * LANES
    n_piece = d // pw

    def up_piece(f, n):
        def run():
            a = jnp.maximum(jnp.dot(hbuf[...], _wb(w1_ref[:, f * d + n * pw:f * d + (n + 1) * pw]),
                                    preferred_element_type=F32), 0.0)
            a2buf[f % 2, :, n * pw:(n + 1) * pw] = (a * a).astype(BF16)
        return run

    def down_piece(f, n):
        def run():
            part = jnp.dot(a2buf[f % 2], _wb(w2_ref[f * d // 2:(f + 1) * d // 2, n * pw:(n + 1) * pw]),
                           preferred_element_type=F32)
            if f == 0:
                yacc[:, n * pw:(n + 1) * pw] = part
            else:
                yacc[:, n * pw:(n + 1) * pw] += part
        return run

    queue = [("up", up_piece(0, n)) for n in range(n_piece)]
    for f in range(n_ff):
        for n in range(n_piece):
            if f + 1 < n_ff:
                queue.append(("up", up_piece(f + 1, n)))
            queue.append(("down", down_piece(f, n)))
    queue.reverse()

    def mlp_step(k=1):
        for _ in range(k):
            if queue:
                queue.pop()[1]()

    mlp_step()
    o_ref[...] = xmid[s % 2] + _rms(yacc[...], mpost_ref[...])

    mlp_step()
    _fill_halo_buf(pbuf, up_ref, upp_ref, upn_ref, i > 0, i < nc - 1)

    def emit(rows, h):
        y = h + hb_ref[rows, :]
        for c in range(ncol):
            qh[c, rows, :] = y[:, c * LANES:(c + 1) * LANES]

    _scan(False, i == 0, af_ref, bf_ref, carry_ref, emit, tick=mlp_step, tick_every=16)

    t_glob = i * tc + lax.broadcasted_iota(jnp.int32, (tc, LANES), 0)
    gw = d // N_POOL_GROUPS
    for g in range(N_POOL_GROUPS):
        cols = pl.ds(g * gw, gw)
        half = POOL_WINDOWS[g] // 2
        count = jnp.minimum(t_glob + half, s_len) - jnp.maximum(t_glob - half, 0)
        inv = 1.0 / count.astype(F32)
        sums = _window_sums(pbuf, t1, t2, g, tc)
        pooled = jnp.concatenate(
            [sums[k] * inv - pbuf[2 * g + k, pl.ds(HALO, tc), :] for k in range(2)], axis=1)
        mlp_step()
        yp = jnp.dot(pooled.astype(BF16), _wb(poolw_ref[g]), preferred_element_type=F32) * pscale_ref[:, cols]
        mlp_step()

        sc = jnp.dot(q_ref[:, cols], kt_ref[cols, :], preferred_element_type=F32)
        e = jnp.exp(sc - jnp.max(sc, axis=-1, keepdims=True))
        p = e * (1.0 / jnp.sum(e, axis=-1, keepdims=True))
        mlp_step()
        ym = jnp.dot(p.astype(BF16), v_ref[:, cols], preferred_element_type=F32)
        mlp_step()

        yl = jnp.concatenate([_unpermute(qh, 2 * g, tc), _unpermute(qh, 2 * g + 1, tc)], axis=1)

        merged = (yp + yl + ym) + jnp.tanh(gl_ref[:, cols]) * yp \
            + jnp.tanh(gl_ref[:, pl.ds(d + g * gw, gw)]) * yl \
            + jnp.tanh(gl_ref[:, pl.ds(2 * d + g * gw, gw)]) * ym
        mbuf[:, cols] = merged.astype(BF16)
        mlp_step()

    rest = len(queue)
    out = jnp.dot(mbuf[...], _wb(wout_ref[...]), preferred_element_type=F32)
    mlp_step(rest // 3)
    xnew = x_ref[...] + _rms(out, gpost_ref[...])
    xmid[s % 2] = xnew
    mlp_step(rest // 3)
    assert all(kind == "down" for kind, _ in queue)
    hbuf[...] = _rms(xnew, mpre_ref[...]).astype(BF16)
    mlp_step(len(queue))


def _mixmlp(x, up, af, bf, hb, q, gl, kt, v, poolw, pscale, wout, gpost, mpre, mpost, w1, w2):
    b, s_len, d = x.shape
    tc = min(TIME_CHUNK, s_len)
    nc = s_len // tc
    total = b * nc
    ncol = d // LANES
    per = tc // HALO
    last_halo = s_len // HALO - 1
    assert w1.shape[1] // d >= N_POOL_GROUPS

    def mix(s):
        c = jnp.minimum(s, total - 1)
        return c // nc, c % nc

    def cur(w):
        return pl.BlockSpec((None, tc, w), lambda s: (*mix(s), 0))

    prev = pl.BlockSpec((None, HALO, d), lambda s: (mix(s)[0], jnp.maximum(mix(s)[1] * per - 1, 0), 0))
    nxt = pl.BlockSpec((None, HALO, d), lambda s: (mix(s)[0], jnp.minimum((mix(s)[1] + 1) * per, last_halo), 0))
    per_batch = lambda a: pl.BlockSpec((None,) + a.shape[1:], lambda s: (mix(s)[0], 0, 0))

    def out_map(s):
        c = jnp.maximum(s - 2, 0)
        return c // nc, c % nc, 0

    scratch = [pltpu.VMEM((ncol, tc + PAD_ROWS, LANES), F32), pltpu.VMEM((ncol, tc, LANES), F32),
               pltpu.VMEM((SUBLANES, d), F32),
               pltpu.VMEM((1, tc + PAD_ROWS, LANES), F32), pltpu.VMEM((1, tc + PAD_ROWS, LANES), F32),
               pltpu.VMEM((tc, d), BF16), pltpu.VMEM((2, tc, d), F32),
               pltpu.VMEM((tc, d), BF16), pltpu.VMEM((2, tc, d), BF16), pltpu.VMEM((tc, d), F32)]
    return pl.pallas_call(
        functools.partial(_mixmlp_body, s_len, nc),
        grid=(total + 2,),
        in_specs=[cur(d), cur(d), prev, nxt, cur(d), cur(d), cur(d), cur(d), cur(3 * d),
                  per_batch(kt), per_batch(v),
                  _resident_spec(poolw.shape), _const_spec(pscale.shape), _resident_spec(wout.shape),
                  _const_spec(gpost.shape), _const_spec(mpre.shape), _const_spec(mpost.shape),
                  _resident_spec(w1.shape), _resident_spec(w2.shape)],
        out_specs=pl.BlockSpec((None, tc, d), out_map),
        out_shape=jax.ShapeDtypeStruct((b, s_len, d), F32),
        scratch_shapes=scratch,
        compiler_params=pltpu.CompilerParams(dimension_semantics=("arbitrary",),
                                             vmem_limit_bytes=VMEM_LIMIT),
        name="mixmlp",
    )(x, up, up, up, af, bf, hb, q, gl, kt, v, poolw, pscale, wout, gpost, mpre, mpost, w1, w2)


def _pack_body(w_ref, s_ref, o_ref):
    o_ref[...] = pltpu.bitcast((w_ref[...] * s_ref[...]).astype(BF16), jnp.uint32)


def _pack_rows(w, col_scale=None):
    nl, k, n = w.shape
    bn = min(n, 4 * LANES)
    scale = jnp.ones((1, n), F32) if col_scale is None else col_scale.reshape(1, n)
    return pl.pallas_call(
        _pack_body,
        grid=(nl, n // bn),
        in_specs=[pl.BlockSpec((None, k, bn), lambda l, j: (l, 0, j)), pl.BlockSpec((1, bn), lambda l, j: (0, j))],
        out_specs=pl.BlockSpec((None, k // 2, bn), lambda l, j: (l, 0, j)),
        out_shape=jax.ShapeDtypeStruct((nl, k // 2, n), jnp.uint32),
        compiler_params=pltpu.CompilerParams(dimension_semantics=("arbitrary", "arbitrary"),
                                             vmem_limit_bytes=VMEM_LIMIT),
        name="packw",
    )(w, scale)


def _wb(packed):
    return pltpu.bitcast(packed, BF16)


def _prep_layers(p):
    nl, d = p["w_in"].shape[:2]
    row = lambda a: a.reshape(1, -1)
    half = lambda n: jnp.full((n,), 0.5, F32)
    w_in = _pack_rows(p["w_in"], jnp.concatenate([jnp.ones((3 * d,), F32), half(3 * d)]))
    wg = jnp.concatenate([p["lru_wa"][:, 0], p["lru_wx"][:, 0], p["lru_wa"][:, 1], p["lru_wx"][:, 1]], axis=-1)
    nb, bk, bn = wg.shape[1:]
    wg = _pack_rows(wg.reshape(nl, nb * bk, bn), half(bn)).reshape(nl, nb, bk // 2, bn)
    ng, gk, gn = p["pool_w"].shape[1:]
    pool_w = _pack_rows(p["pool_w"].reshape(nl, ng * gk, gn)).reshape(nl, ng, gk // 2, gn)
    w_out = _pack_rows(p["w_out"], half(d))
    w1 = _pack_rows(p["mlp_w1"])
    w2 = _pack_rows(p["mlp_w2"])
    layers = []
    for l in range(nl):
        bias = 0.5 * jnp.stack([p["lru_ba"][l, 0], p["lru_bx"][l, 0], p["lru_ba"][l, 1], p["lru_bx"][l, 1]])
        layers.append(dict(
            g_pre=row(p["norm_mix_pre"][l]), g_post=row(p["norm_mix_post"][l]), g_mem=row(p["norm_mem"][l]),
            w_in=w_in[l], pool_w=pool_w[l], pool_scale=row(p["pool_scale"][l]),
            conv_w=p["conv_w"][l], conv_b=row(p["conv_b"][l]), wg=wg[l], bias=bias, lam=p["lru_lambda"][l],
            wk=p["w_kv"][l][:, :d].astype(BF16), wv=p["w_kv"][l][:, d:].astype(BF16),
            w_out=w_out[l],
            m_pre=row(p["norm_mlp_pre"][l]), m_post=row(p["norm_mlp_post"][l]),
            w1=w1[l], w2=w2[l],
        ))
    return layers


def _trunk(x, mem, layers):
    b, s_len, d = x.shape
    for lp in layers:
        up, q, gl, af, bf, hb = _inproj(x, lp["g_pre"], lp["w_in"], lp["conv_w"], lp["conv_b"],
                                        lp["wg"], lp["bias"], lp["lam"])
        kt, v = _memkv(mem, lp["g_mem"], lp["wk"], lp["wv"])
        x = _mixmlp(x, up, af, bf, hb, q, gl, kt, v, lp["pool_w"], lp["pool_scale"], lp["w_out"], lp["g_post"],
                    lp["m_pre"], lp["m_post"], lp["w1"], lp["w2"])
    return x


def kernel(x_prompt, x_sample, mem_prompt, mem_sample, norm_mix_pre, norm_mix_post, norm_mem, w_in, pool_w, pool_scale, conv_w, conv_b, lru_wa, lru_ba, lru_wx, lru_bx, lru_lambda, w_kv, w_out, norm_mlp_pre, norm_mlp_post, mlp_w1, mlp_w2):
    p = dict(norm_mix_pre=norm_mix_pre, norm_mix_post=norm_mix_post, norm_mem=norm_mem, w_in=w_in,
             pool_w=pool_w, pool_scale=pool_scale, conv_w=conv_w, conv_b=conv_b, lru_wa=lru_wa,
             lru_ba=lru_ba, lru_wx=lru_wx, lru_bx=lru_bx, lru_lambda=lru_lambda, w_kv=w_kv, w_out=w_out,
             norm_mlp_pre=norm_mlp_pre, norm_mlp_post=norm_mlp_post, mlp_w1=mlp_w1, mlp_w2=mlp_w2)
    layers = _prep_layers(p)
    return (_trunk(x_prompt, mem_prompt, layers), _trunk(x_sample, mem_sample, layers))
```

```python
import functools

import jax
import jax.numpy as jnp
from jax import lax
from jax.experimental import pallas as pl
from jax.experimental.pallas import tpu as pltpu

F32 = jnp.float32
BF16 = jnp.bfloat16

LANES = 128
SUBLANES = 8
EPS = 1e-6
LRU_C = 8.0
POOL_WINDOWS = (2, 4, 8, 16)
N_POOL_GROUPS = len(POOL_WINDOWS)
N_XHEADS = 4
CONV_WIDTH = 4
CONV_LEFT = 2
HALO = 8
PAD_ROWS = 32
VMEM_LIMIT = 56 * 1024 * 1024

TIME_CHUNK = 256


def _rms(x, gain):
    ms = jnp.mean(x * x, axis=-1, keepdims=True)
    return x * lax.rsqrt(ms + EPS) * gain


def _const_spec(shape):
    nd = len(shape)
    return pl.BlockSpec(shape, lambda *_: (0,) * nd)


def _resident_spec(shape):
    nd = len(shape)
    return pl.BlockSpec(shape, lambda *_: (0,) * nd, pipeline_mode=pl.Buffered(1))


def _shifted(buf, col, start, rows):
    return buf[pl.ds(col, 1, stride=2), pl.ds(start, rows), :].reshape(rows, LANES)


def _halo_specs(tc, d, s_len, chunk_of):
    per = tc // HALO
    last = s_len // HALO - 1
    cur = pl.BlockSpec((None, tc, d), lambda b, i: (b, chunk_of(i), 0))
    prev = pl.BlockSpec((None, HALO, d), lambda b, i: (b, jnp.maximum(chunk_of(i) * per - 1, 0), 0))
    nxt = pl.BlockSpec((None, HALO, d), lambda b, i: (b, jnp.minimum((chunk_of(i) + 1) * per, last), 0))
    return [cur, prev, nxt]


def _reset_carry(first_step, carry_ref):
    @pl.when(first_step)
    def _():
        carry_ref[...] = jnp.zeros(carry_ref.shape, F32)


def _scan(reverse, a_ref, b_ref, carry_ref, emit, tick=lambda: None, tick_every=8):
    tc, d = a_ref.shape
    sub = tc // SUBLANES

    def step(s):
        k = (sub - 1 - s) if reverse else s
        return pl.ds(k * SUBLANES, SUBLANES)

    shape = (SUBLANES, d)
    f, pf = jnp.zeros(shape, F32), jnp.ones(shape, F32)
    for s in range(sub):
        rows = step(s)
        a = a_ref[rows, :]
        f, pf = a * f + b_ref[rows, :], a * pf
        if (s + 1) % tick_every == 0:
            tick()

    row = lax.broadcasted_iota(jnp.int32, shape, 0)
    edge = (SUBLANES - 1) if reverse else 0
    shift = (SUBLANES - 1) if reverse else 1
    carry = carry_ref[...]
    init = carry
    for _ in range(SUBLANES - 1):
        init = jnp.where(row == edge, carry, pltpu.roll(f + pf * init, shift, 0))
    last = f + pf * init
    out_row = 0 if reverse else SUBLANES - 1
    carry_ref[...] = jnp.broadcast_to(last[out_row:out_row + 1, :], shape)
    tick()

    h = init
    for s in range(sub):
        rows = step(s)
        h = a_ref[rows, :] * h + b_ref[rows, :]
        emit(rows, h)
        if (s + 1) % tick_every == 0:
            tick()


def _inproj_body(x_ref, xp_ref, xn_ref, g_ref, w_ref, convw_ref, convb_ref, wg_ref, bias_ref, lam_ref,
                 up_ref, q_ref, gl_ref, af_ref, bf_ref, hb_ref,
                 ubuf, xq, qa, qb, carry_ref):
    i = pl.program_id(1)
    nc = pl.num_programs(1)
    ti = nc - 1 - i
    tc, d = x_ref.shape
    ncol = d // LANES
    sub = tc // SUBLANES
    _reset_carry(i == 0, carry_ref)
    gain = g_ref[...]

    hc = _rms(x_ref[...], gain)
    hcb = hc.astype(BF16)

    hall = jnp.concatenate([_rms(xp_ref[...], gain), hc, _rms(xn_ref[...], gain)], axis=0).astype(BF16)
    ul = jnp.dot(hall, _wb(w_ref[:, d:2 * d]), preferred_element_type=F32)
    for c in range(ncol):
        cs = slice(c * LANES, (c + 1) * LANES)
        ubuf[c, pl.ds(0, HALO), :] = jnp.where(ti > 0, ul[0:HALO, cs], 0.0)
        ubuf[c, pl.ds(HALO, tc), :] = ul[HALO:HALO + tc, cs]
        ubuf[c, pl.ds(HALO + tc, HALO), :] = jnp.where(ti < nc - 1, ul[HALO + tc:, cs], 0.0)

    wide = [(up_ref, 0, 0, d), (q_ref, 0, 2 * d, d), (gl_ref, 0, 3 * d, 3 * d)]
    chunk = 4 * LANES
    wide_chunks = [(ref, o + k, col + k) for ref, o, col, n in wide for k in range(0, n, chunk)]

    def wide_dot(ref, o, col):
        r = jnp.dot(hcb, _wb(w_ref[:, col:col + chunk]), preferred_element_type=F32)
        ref[:, o:o + chunk] = r.astype(ref.dtype)

    assert len(wide_chunks) >= ncol
    for c in range(ncol):
        cs = pl.ds(c * LANES, LANES)
        w = convw_ref[:, cs]
        xc = convb_ref[:, cs] + _shifted(ubuf, c, HALO - CONV_LEFT, tc) * w[0:1]
        for k in range(1, CONV_WIDTH):
            xc = xc + _shifted(ubuf, c, HALO - CONV_LEFT + k, tc) * w[k:k + 1]
        for j in range(SUBLANES):
            xq[c, pl.ds(j, sub, stride=SUBLANES), :] = xc[j * sub:(j + 1) * sub]
        wide_dot(*wide_chunks[c])

        xb = xq[c]
        g4 = jnp.dot(xb.astype(BF16), _wb(wg_ref[c]), preferred_element_type=F32)
        for dirn in range(2):
            o = 2 * dirn * LANES
            tr = jnp.tanh(g4[:, o:o + LANES] + bias_ref[2 * dirn:2 * dirn + 1, cs])
            px = g4[:, o + LANES:o + 2 * LANES] + bias_ref[2 * dirn + 1:2 * dirn + 2, cs]
            tx = jnp.tanh(px.astype(BF16)).astype(F32)
            nk = (-0.5 * LRU_C) * jax.nn.softplus(-lam_ref[dirn:dirn + 1, cs])
            log_a = nk * tr + nk
            a = jnp.exp(log_a)
            m2 = jnp.tanh(log_a) * (-1.0 - a * a)
            mult = m2 * lax.rsqrt(jnp.maximum(m2, 1e-30))
            b = mult * (0.5 * tx + 0.5) * xb
            if dirn == 0:
                af_ref[:, cs] = a
                bf_ref[:, cs] = b
            else:
                qa[:, cs] = a
                qb[:, cs] = b

    def emit(rows, h):
        hb_ref[rows, :] = h

    rest = wide_chunks[ncol:]
    for args in rest[:len(rest) // 2]:
        wide_dot(*args)
    _scan(True, qa, qb, carry_ref, emit)
    for args in rest[len(rest) // 2:]:
        wide_dot(*args)


def _inproj(x, gain, w_in, convw, convb, wg, bias, lam):
    b, s_len, d = x.shape
    tc = min(TIME_CHUNK, s_len)
    nc = s_len // tc
    ncol = d // LANES
    rev = lambda i: nc - 1 - i
    out = lambda w: pl.BlockSpec((None, tc, w), lambda bb, i: (bb, rev(i), 0))
    f32_out = lambda w: jax.ShapeDtypeStruct((b, s_len, w), F32)
    return pl.pallas_call(
        _inproj_body,
        grid=(b, nc),
        in_specs=_halo_specs(tc, d, s_len, rev) + [
            _const_spec(gain.shape), _resident_spec(w_in.shape), _const_spec(convw.shape),
            _const_spec(convb.shape), _const_spec(wg.shape), _const_spec(bias.shape), _const_spec(lam.shape)],
        out_specs=[out(d), out(d), out(3 * d), out(d), out(d), out(d)],
        out_shape=[f32_out(d), jax.ShapeDtypeStruct((b, s_len, d), BF16), f32_out(3 * d),
                   f32_out(d), f32_out(d), f32_out(d)],
        scratch_shapes=[pltpu.VMEM((ncol, tc + 2 * HALO, LANES), F32), pltpu.VMEM((ncol, tc, LANES), F32),
                        pltpu.VMEM((tc, d), F32), pltpu.VMEM((tc, d), F32), pltpu.VMEM((SUBLANES, d), F32)],
        compiler_params=pltpu.CompilerParams(dimension_semantics=("arbitrary", "arbitrary"),
                                             vmem_limit_bytes=VMEM_LIMIT),
        name="inproj",
    )(x, x, x, gain, w_in, convw, convb, wg, bias, lam)


def _memkv_body(scale, mem_ref, g_ref, wk_ref, wv_ref, kt_ref, v_ref):
    mn = _rms(mem_ref[...], g_ref[...]).astype(BF16)
    k = jnp.dot(mn, wk_ref[...], preferred_element_type=F32)
    kt_ref[...] = (k.T * scale).astype(BF16)
    v_ref[...] = jnp.dot(mn, wv_ref[...], preferred_element_type=F32).astype(BF16)


def _memkv(mem, gain, wk, wv):
    b, m, d = mem.shape
    scale = float((d // N_XHEADS) ** -0.5)
    return pl.pallas_call(
        functools.partial(_memkv_body, scale),
        grid=(b,),
        in_specs=[pl.BlockSpec((None, m, d), lambda i: (i, 0, 0)), _const_spec((1, d)),
                  _const_spec(wk.shape), _const_spec(wv.shape)],
        out_specs=[pl.BlockSpec((None, d, m), lambda i: (i, 0, 0)),
                   pl.BlockSpec((None, m, d), lambda i: (i, 0, 0))],
        out_shape=[jax.ShapeDtypeStruct((b, d, m), BF16), jax.ShapeDtypeStruct((b, m, d), BF16)],
        compiler_params=pltpu.CompilerParams(dimension_semantics=("arbitrary",),
                                             vmem_limit_bytes=VMEM_LIMIT),
        name="memkv",
    )(mem, gain, wk, wv)


def _fill_halo_buf(buf, cur_ref, prev_ref, next_ref, has_prev, has_next):
    tc = cur_ref.shape[0]
    ncol = cur_ref.shape[1] // LANES
    for c in range(ncol):
        cs = pl.ds(c * LANES, LANES)
        buf[c, pl.ds(0, HALO), :] = jnp.where(has_prev, prev_ref[:, cs], 0.0)
        buf[c, pl.ds(HALO, tc), :] = cur_ref[:, cs]
        buf[c, pl.ds(HALO + tc, HALO), :] = jnp.where(has_next, next_ref[:, cs], 0.0)
        buf[c, pl.ds(2 * HALO + tc, PAD_ROWS - 2 * HALO), :] = jnp.zeros((PAD_ROWS - 2 * HALO, LANES), F32)


def _unpermute(qh, col, tc):
    sub = tc // SUBLANES
    return jnp.concatenate(
        [qh[col, pl.ds(j, sub, stride=SUBLANES), :] for j in range(SUBLANES)], axis=0)


def _window_sums(pbuf, t1, t2, group, tc):
    outs = []
    for c in (2 * group, 2 * group + 1):
        if group == 0:
            outs.append(_shifted(pbuf, c, 7, tc) + pbuf[c, pl.ds(8, tc), :])
        elif group == 1:
            outs.append((_shifted(pbuf, c, 6, tc) + _shifted(pbuf, c, 7, tc))
                        + (pbuf[c, pl.ds(8, tc), :] + _shifted(pbuf, c, 9, tc)))
        else:
            n2 = tc + 24 if group == 3 else tc + 16
            t1[0, pl.ds(0, n2), :] = pbuf[c, pl.ds(0, n2), :] + _shifted(pbuf, c, 1, n2)
            n4 = n2 - 8
            t2[0, pl.ds(0, n4), :] = t1[0, pl.ds(0, n4), :] + _shifted(t1, 0, 2, n4)
            if group == 2:
                outs.append(_shifted(t2, 0, 4, tc) + t2[0, pl.ds(8, tc), :])
            else:
                n8 = n4 - 8
                t1[0, pl.ds(0, n8), :] = t2[0, pl.ds(0, n8), :] + _shifted(t2, 0, 4, n8)
                outs.append(t1[0, pl.ds(0, tc), :] + t1[0, pl.ds(8, tc), :])
    return outs


def _mixmlp_body(s_len, nc, x_ref, up_ref, upp_ref, upn_ref, af_ref, bf_ref, hb_ref, q_ref, gl_ref,
                 kt_ref, v_ref, poolw_ref, pscale_ref, wout_ref, gpost_ref, mpre_ref, mpost_ref,
                 w1_ref, w2_ref, o_ref,
                 pbuf, qh, carry_ref, t1, t2, mbuf, xmid, hbuf, a2buf, yacc):
    s = pl.program_id(0)
    last = pl.num_programs(0) - 3
    i = jnp.minimum(s, last) % nc
    tc, d = x_ref.shape
    ncol = d // LANES
    n_ff = w1_ref.shape[1] // d

    @pl.when(s == 0)
    def _():
        xmid[...] = jnp.zeros(xmid.shape, F32)
        hbuf[...] = jnp.zeros(hbuf.shape, BF16)
        yacc[...] = jnp.zeros(yacc.shape, F32)

    _reset_carry(i == 0, carry_ref)

    pw = 2 * LANES
    n_piece = d // pw

    def up_piece(f, n):
        def run():
            a = jnp.maximum(jnp.dot(hbuf[...], _wb(w1_ref[:, f * d + n * pw:f * d + (n + 1) * pw]),
                                    preferred_element_type=F32), 0.0)
            a2buf[f % 2, :, n * pw:(n + 1) * pw] = (a * a).astype(BF16)
        return run

    def down_piece(f, n):
        def run():
            part = jnp.dot(a2buf[f % 2], _wb(w2_ref[f * d // 2:(f + 1) * d // 2, n * pw:(n + 1) * pw]),
                           preferred_element_type=F32)
            if f == 0:
                yacc[:, n * pw:(n + 1) * pw] = part
            else:
                yacc[:, n * pw:(n + 1) * pw] += part
        return run

    queue = [("up", up_piece(0, n)) for n in range(n_piece)]
    for f in range(n_ff):
        for n in range(n_piece):
            if f + 1 < n_ff:
                queue.append(("up", up_piece(f + 1, n)))
            queue.append(("down", down_piece(f, n)))
    queue.reverse()

    def mlp_step(k=1):
        for _ in range(k):
            if queue:
                queue.pop()[1]()

    mlp_step()
    o_ref[...] = xmid[s % 2] + _rms(yacc[...], mpost_ref[...])

    mlp_step()
    _fill_halo_buf(pbuf, up_ref, upp_ref, upn_ref, i > 0, i < nc - 1)

    def emit(rows, h):
        y = h + hb_ref[rows, :]
        for c in range(ncol):
            qh[c, rows, :] = y[:, c * LANES:(c + 1) * LANES]

    _scan(False, af_ref, bf_ref, carry_ref, emit, tick=mlp_step, tick_every=16)

    t_glob = i * tc + lax.broadcasted_iota(jnp.int32, (tc, LANES), 0)
    gw = d // N_POOL_GROUPS
    for g in range(N_POOL_GROUPS):
        cols = pl.ds(g * gw, gw)
        half = POOL_WINDOWS[g] // 2
        count = jnp.minimum(t_glob + half, s_len) - jnp.maximum(t_glob - half, 0)
        inv = 1.0 / count.astype(F32)
        sums = _window_sums(pbuf, t1, t2, g, tc)
        pooled = jnp.concatenate(
            [sums[k] * inv - pbuf[2 * g + k, pl.ds(HALO, tc), :] for k in range(2)], axis=1)
        mlp_step()
        yp = jnp.dot(pooled.astype(BF16), _wb(poolw_ref[g]), preferred_element_type=F32) * pscale_ref[:, cols]
        mlp_step()

        sc = jnp.dot(q_ref[:, cols], kt_ref[cols, :], preferred_element_type=F32)
        e = jnp.exp(sc - jnp.max(sc, axis=-1, keepdims=True))
        p = e * (1.0 / jnp.sum(e, axis=-1, keepdims=True))
        mlp_step()
        ym = jnp.dot(p.astype(BF16), v_ref[:, cols], preferred_element_type=F32)
        mlp_step()

        yl = jnp.concatenate([_unpermute(qh, 2 * g, tc), _unpermute(qh, 2 * g + 1, tc)], axis=1)

        gate = lambda b: jnp.tanh(gl_ref[:, pl.ds(b * d + g * gw, gw)].astype(BF16)).astype(F32)
        merged = (yp + yl + ym) + gate(0) * yp + gate(1) * yl + gate(2) * ym
        mbuf[:, cols] = merged.astype(BF16)
        mlp_step()

    rest = len(queue)
    out = jnp.dot(mbuf[...], _wb(wout_ref[...]), preferred_element_type=F32)
    mlp_step(rest // 3)
    xnew = x_ref[...] + _rms(out, gpost_ref[...])
    xmid[s % 2] = xnew
    mlp_step(rest // 3)
    assert all(kind == "down" for kind, _ in queue)
    hbuf[...] = _rms(xnew, mpre_ref[...]).astype(BF16)
    mlp_step(len(queue))


def _mixmlp(x, up, af, bf, hb, q, gl, kt, v, poolw, pscale, wout, gpost, mpre, mpost, w1, w2):
    b, s_len, d = x.shape
    tc = min(TIME_CHUNK, s_len)
    nc = s_len // tc
    total = b * nc
    ncol = d // LANES
    per = tc // HALO
    last_halo = s_len // HALO - 1
    assert w1.shape[1] // d >= N_POOL_GROUPS

    def mix(s):
        c = jnp.minimum(s, total - 1)
        return c // nc, c % nc

    def cur(w):
        return pl.BlockSpec((None, tc, w), lambda s: (*mix(s), 0))

    prev = pl.BlockSpec((None, HALO, d), lambda s: (mix(s)[0], jnp.maximum(mix(s)[1] * per - 1, 0), 0))
    nxt = pl.BlockSpec((None, HALO, d), lambda s: (mix(s)[0], jnp.minimum((mix(s)[1] + 1) * per, last_halo), 0))
    per_batch = lambda a: pl.BlockSpec((None,) + a.shape[1:], lambda s: (mix(s)[0], 0, 0))

    def out_map(s):
        c = jnp.maximum(s - 2, 0)
        return c // nc, c % nc, 0

    scratch = [pltpu.VMEM((ncol, tc + PAD_ROWS, LANES), F32), pltpu.VMEM((ncol, tc, LANES), F32),
               pltpu.VMEM((SUBLANES, d), F32),
               pltpu.VMEM((1, tc + PAD_ROWS, LANES), F32), pltpu.VMEM((1, tc + PAD_ROWS, LANES), F32),
               pltpu.VMEM((tc, d), BF16), pltpu.VMEM((2, tc, d), F32),
               pltpu.VMEM((tc, d), BF16), pltpu.VMEM((2, tc, d), BF16), pltpu.VMEM((tc, d), F32)]
    return pl.pallas_call(
        functools.partial(_mixmlp_body, s_len, nc),
        grid=(total + 2,),
        in_specs=[cur(d), cur(d), prev, nxt, cur(d), cur(d), cur(d), cur(d), cur(3 * d),
                  per_batch(kt), per_batch(v),
                  _resident_spec(poolw.shape), _const_spec(pscale.shape), _resident_spec(wout.shape),
                  _const_spec(gpost.shape), _const_spec(mpre.shape), _const_spec(mpost.shape),
                  _resident_spec(w1.shape), _resident_spec(w2.shape)],
        out_specs=pl.BlockSpec((None, tc, d), out_map),
        out_shape=jax.ShapeDtypeStruct((b, s_len, d), F32),
        scratch_shapes=scratch,
        compiler_params=pltpu.CompilerParams(dimension_semantics=("arbitrary",),
                                             vmem_limit_bytes=VMEM_LIMIT),
        name="mixmlp",
    )(x, up, up, up, af, bf, hb, q, gl, kt, v, poolw, pscale, wout, gpost, mpre, mpost, w1, w2)


def _pack_body(w_ref, s_ref, o_ref):
    o_ref[...] = pltpu.bitcast((w_ref[...] * s_ref[...]).astype(BF16), jnp.uint32)


def _pack_rows(w, col_scale=None):
    nl, k, n = w.shape
    bn = min(n, 4 * LANES)
    scale = jnp.ones((1, n), F32) if col_scale is None else col_scale.reshape(1, n)
    return pl.pallas_call(
        _pack_body,
        grid=(nl, n // bn),
        in_specs=[pl.BlockSpec((None, k, bn), lambda l, j: (l, 0, j)), pl.BlockSpec((1, bn), lambda l, j: (0, j))],
        out_specs=pl.BlockSpec((None, k // 2, bn), lambda l, j: (l, 0, j)),
        out_shape=jax.ShapeDtypeStruct((nl, k // 2, n), jnp.uint32),
        compiler_params=pltpu.CompilerParams(dimension_semantics=("arbitrary", "arbitrary"),
                                             vmem_limit_bytes=VMEM_LIMIT),
        name="packw",
    )(w, scale)


def _wb(packed):
    return pltpu.bitcast(packed, BF16)


def _prep_layers(p):
    nl, d = p["w_in"].shape[:2]
    row = lambda a: a.reshape(1, -1)
    half = lambda n: jnp.full((n,), 0.5, F32)
    w_in = _pack_rows(p["w_in"], jnp.concatenate([jnp.ones((3 * d,), F32), half(3 * d)]))
    wg = jnp.concatenate([p["lru_wa"][:, 0], p["lru_wx"][:, 0], p["lru_wa"][:, 1], p["lru_wx"][:, 1]], axis=-1)
    nb, bk, bn = wg.shape[1:]
    wg = _pack_rows(wg.reshape(nl, nb * bk, bn), half(bn)).reshape(nl, nb, bk // 2, bn)
    ng, gk, gn = p["pool_w"].shape[1:]
    pool_w = _pack_rows(p["pool_w"].reshape(nl, ng * gk, gn)).reshape(nl, ng, gk // 2, gn)
    w_out = _pack_rows(p["w_out"], half(d))
    w1 = _pack_rows(p["mlp_w1"])
    w2 = _pack_rows(p["mlp_w2"])
    layers = []
    for l in range(nl):
        bias = 0.5 * jnp.stack([p["lru_ba"][l, 0], p["lru_bx"][l, 0], p["lru_ba"][l, 1], p["lru_bx"][l, 1]])
        layers.append(dict(
            g_pre=row(p["norm_mix_pre"][l]), g_post=row(p["norm_mix_post"][l]), g_mem=row(p["norm_mem"][l]),
            w_in=w_in[l], pool_w=pool_w[l], pool_scale=row(p["pool_scale"][l]),
            conv_w=p["conv_w"][l], conv_b=row(p["conv_b"][l]), wg=wg[l], bias=bias, lam=p["lru_lambda"][l],
            wk=p["w_kv"][l][:, :d].astype(BF16), wv=p["w_kv"][l][:, d:].astype(BF16),
            w_out=w_out[l],
            m_pre=row(p["norm_mlp_pre"][l]), m_post=row(p["norm_mlp_post"][l]),
            w1=w1[l], w2=w2[l],
        ))
    return layers


def _trunk(x, mem, layers):
    b, s_len, d = x.shape
    for lp in layers:
        up, q, gl, af, bf, hb = _inproj(x, lp["g_pre"], lp["w_in"], lp["conv_w"], lp["conv_b"],
                                        lp["wg"], lp["bias"], lp["lam"])
        kt, v = _memkv(mem, lp["g_mem"], lp["wk"], lp["wv"])
        x = _mixmlp(x, up, af, bf, hb, q, gl, kt, v, lp["pool_w"], lp["pool_scale"], lp["w_out"], lp["g_post"],
                    lp["m_pre"], lp["m_post"], lp["w1"], lp["w2"])
    return x


def kernel(x_prompt, x_sample, mem_prompt, mem_sample, norm_mix_pre, norm_mix_post, norm_mem, w_in, pool_w, pool_scale, conv_w, conv_b, lru_wa, lru_ba, lru_wx, lru_bx, lru_lambda, w_kv, w_out, norm_mlp_pre, norm_mlp_post, mlp_w1, mlp_w2):
    p = dict(norm_mix_pre=norm_mix_pre, norm_mix_post=norm_mix_post, norm_mem=norm_mem, w_in=w_in,
             pool_w=pool_w, pool_scale=pool_scale, conv_w=conv_w, conv_b=conv_b, lru_wa=lru_wa,
             lru_ba=lru_ba, lru_wx=lru_wx, lru_bx=lru_bx, lru_lambda=lru_lambda, w_kv=w_kv, w_out=w_out,
             norm_mlp_pre=norm_mlp_pre, norm_mlp_post=norm_mlp_post, mlp_w1=mlp_w1, mlp_w2=mlp_w2)
    layers = _prep_layers(p)
    return (_trunk(x_prompt, mem_prompt, layers), _trunk(x_sample, mem_sample, layers))
```

```python
import functools

import jax
import jax.numpy as jnp
from jax import lax
from jax.experimental import pallas as pl
from jax.experimental.pallas import tpu as pltpu

F32 = jnp.float32
BF16 = jnp.bfloat16

LANES = 128
SUBLANES = 8
EPS = 1e-6
LRU_C = 8.0
POOL_WINDOWS = (2, 4, 8, 16)
N_POOL_GROUPS = len(POOL_WINDOWS)
N_XHEADS = 4
CONV_WIDTH = 4
CONV_LEFT = 2
HALO = 8
PAD_ROWS = 32
VMEM_LIMIT = 56 * 1024 * 1024

TIME_CHUNK = 256


def _rms(x, gain):
    ms = jnp.mean(x * x, axis=-1, keepdims=True)
    return x * lax.rsqrt(ms + EPS) * gain


def _const_spec(shape):
    nd = len(shape)
    return pl.BlockSpec(shape, lambda *_: (0,) * nd)


def _resident_spec(shape):
    nd = len(shape)
    return pl.BlockSpec(shape, lambda *_: (0,) * nd, pipeline_mode=pl.Buffered(1))


def _shifted(buf, col, start, rows):
    return buf[pl.ds(col, 1, stride=2), pl.ds(start, rows), :].reshape(rows, LANES)


def _halo_specs(tc, d, s_len, chunk_of):
    per = tc // HALO
    last = s_len // HALO - 1
    cur = pl.BlockSpec((None, tc, d), lambda b, i: (b, chunk_of(i), 0))
    prev = pl.BlockSpec((None, HALO, d), lambda b, i: (b, jnp.maximum(chunk_of(i) * per - 1, 0), 0))
    nxt = pl.BlockSpec((None, HALO, d), lambda b, i: (b, jnp.minimum((chunk_of(i) + 1) * per, last), 0))
    return [cur, prev, nxt]


def _reset_carry(first_step, carry_ref):
    @pl.when(first_step)
    def _():
        carry_ref[...] = jnp.zeros(carry_ref.shape, F32)


def _scan(reverse, a_ref, b_ref, carry_ref, emit, tick=lambda: None, tick_every=8):
    tc, d = a_ref.shape
    sub = tc // SUBLANES

    def step(s):
        k = (sub - 1 - s) if reverse else s
        return pl.ds(k * SUBLANES, SUBLANES)

    shape = (SUBLANES, d)
    f, pf = jnp.zeros(shape, F32), jnp.ones(shape, F32)
    for s in range(sub):
        rows = step(s)
        a = a_ref[rows, :]
        f, pf = a * f + b_ref[rows, :], a * pf
        if (s + 1) % tick_every == 0:
            tick()

    row = lax.broadcasted_iota(jnp.int32, shape, 0)
    edge = (SUBLANES - 1) if reverse else 0
    shift = (SUBLANES - 1) if reverse else 1
    carry = carry_ref[...]
    init = carry
    for _ in range(SUBLANES - 1):
        init = jnp.where(row == edge, carry, pltpu.roll(f + pf * init, shift, 0))
    last = f + pf * init
    out_row = 0 if reverse else SUBLANES - 1
    carry_ref[...] = jnp.broadcast_to(last[out_row:out_row + 1, :], shape)
    tick()

    h = init
    for s in range(sub):
        rows = step(s)
        h = a_ref[rows, :] * h + b_ref[rows, :]
        emit(rows, h)
        if (s + 1) % tick_every == 0:
            tick()


def _inproj_body(x_ref, xp_ref, xn_ref, g_ref, w_ref, convw_ref, convb_ref, wg_ref, bias_ref, lam_ref,
                 up_ref, q_ref, gl_ref, af_ref, bf_ref, hb_ref,
                 ubuf, xq, qa, qb, carry_ref):
    i = pl.program_id(1)
    nc = pl.num_programs(1)
    ti = nc - 1 - i
    tc, d = x_ref.shape
    ncol = d // LANES
    sub = tc // SUBLANES
    _reset_carry(i == 0, carry_ref)
    gain = g_ref[...]

    hc = _rms(x_ref[...], gain)
    hcb = hc.astype(BF16)

    hall = jnp.concatenate([_rms(xp_ref[...], gain), hc, _rms(xn_ref[...], gain)], axis=0).astype(BF16)
    ul = jnp.dot(hall, _wb(w_ref[:, d:2 * d]), preferred_element_type=F32)
    for c in range(ncol):
        cs = slice(c * LANES, (c + 1) * LANES)
        ubuf[c, pl.ds(0, HALO), :] = jnp.where(ti > 0, ul[0:HALO, cs], 0.0)
        ubuf[c, pl.ds(HALO, tc), :] = ul[HALO:HALO + tc, cs]
        ubuf[c, pl.ds(HALO + tc, HALO), :] = jnp.where(ti < nc - 1, ul[HALO + tc:, cs], 0.0)

    wide = [(up_ref, 0, 0, d), (q_ref, 0, 2 * d, d), (gl_ref, 0, 3 * d, 3 * d)]
    chunk = 4 * LANES
    wide_chunks = [(ref, o + k, col + k) for ref, o, col, n in wide for k in range(0, n, chunk)]

    def wide_dot(ref, o, col):
        r = jnp.dot(hcb, _wb(w_ref[:, col:col + chunk]), preferred_element_type=F32)
        ref[:, o:o + chunk] = r.astype(ref.dtype)

    assert len(wide_chunks) >= ncol
    for c in range(ncol):
        cs = pl.ds(c * LANES, LANES)
        w = convw_ref[:, cs]
        xc = convb_ref[:, cs] + _shifted(ubuf, c, HALO - CONV_LEFT, tc) * w[0:1]
        for k in range(1, CONV_WIDTH):
            xc = xc + _shifted(ubuf, c, HALO - CONV_LEFT + k, tc) * w[k:k + 1]
        for j in range(SUBLANES):
            xq[c, pl.ds(j, sub, stride=SUBLANES), :] = xc[j * sub:(j + 1) * sub]
        wide_dot(*wide_chunks[c])

        xb = xq[c]
        g4 = jnp.dot(xb.astype(BF16), _wb(wg_ref[c]), preferred_element_type=F32)
        for dirn in range(2):
            o = 2 * dirn * LANES
            tr = jnp.tanh(g4[:, o:o + LANES] + bias_ref[2 * dirn:2 * dirn + 1, cs])
            tx = jnp.tanh(g4[:, o + LANES:o + 2 * LANES] + bias_ref[2 * dirn + 1:2 * dirn + 2, cs])
            nk = (-0.5 * LRU_C) * jax.nn.softplus(-lam_ref[dirn:dirn + 1, cs])
            log_a = nk * tr + nk
            a = jnp.exp(log_a)
            m2 = jnp.tanh(log_a) * (-1.0 - a * a)
            mult = m2 * lax.rsqrt(jnp.maximum(m2, 1e-30))
            b = mult * (0.5 * tx + 0.5) * xb
            if dirn == 0:
                af_ref[:, cs] = a
                bf_ref[:, cs] = b
            else:
                qa[:, cs] = a
                qb[:, cs] = b

    def emit(rows, h):
        hb_ref[rows, :] = h

    rest = wide_chunks[ncol:]
    for args in rest[:len(rest) // 2]:
        wide_dot(*args)
    _scan(True, qa, qb, carry_ref, emit)
    for args in rest[len(rest) // 2:]:
        wide_dot(*args)


def _inproj(x, gain, w_in, convw, convb, wg, bias, lam):
    b, s_len, d = x.shape
    tc = min(TIME_CHUNK, s_len)
    nc = s_len // tc
    ncol = d // LANES
    rev = lambda i: nc - 1 - i
    out = lambda w: pl.BlockSpec((None, tc, w), lambda bb, i: (bb, rev(i), 0))
    f32_out = lambda w: jax.ShapeDtypeStruct((b, s_len, w), F32)
    return pl.pallas_call(
        _inproj_body,
        grid=(b, nc),
        in_specs=_halo_specs(tc, d, s_len, rev) + [
            _const_spec(gain.shape), _resident_spec(w_in.shape), _const_spec(convw.shape),
            _const_spec(convb.shape), _const_spec(wg.shape), _const_spec(bias.shape), _const_spec(lam.shape)],
        out_specs=[out(d), out(d), out(3 * d), out(d), out(d), out(d)],
        out_shape=[f32_out(d), jax.ShapeDtypeStruct((b, s_len, d), BF16), f32_out(3 * d),
                   f32_out(d), f32_out(d), f32_out(d)],
        scratch_shapes=[pltpu.VMEM((ncol, tc + 2 * HALO, LANES), F32), pltpu.VMEM((ncol, tc, LANES), F32),
                        pltpu.VMEM((tc, d), F32), pltpu.VMEM((tc, d), F32), pltpu.VMEM((SUBLANES, d), F32)],
        compiler_params=pltpu.CompilerParams(dimension_semantics=("arbitrary", "arbitrary"),
                                             vmem_limit_bytes=VMEM_LIMIT),
        name="inproj",
    )(x, x, x, gain, w_in, convw, convb, wg, bias, lam)


def _memkv_body(scale, mem_ref, g_ref, wk_ref, wv_ref, kt_ref, v_ref):
    mn = _rms(mem_ref[...], g_ref[...]).astype(BF16)
    k = jnp.dot(mn, wk_ref[...], preferred_element_type=F32)
    kt_ref[...] = pltpu.bitcast((k.T * scale).astype(BF16), jnp.uint32)
    v = jnp.dot(mn, wv_ref[...], preferred_element_type=F32).astype(BF16)
    v_ref[...] = pltpu.bitcast(v, jnp.uint32)


def _memkv(mem, gain, wk, wv):
    b, m, d = mem.shape
    scale = float((d // N_XHEADS) ** -0.5)
    return pl.pallas_call(
        functools.partial(_memkv_body, scale),
        grid=(b,),
        in_specs=[pl.BlockSpec((None, m, d), lambda i: (i, 0, 0)), _const_spec((1, d)),
                  _const_spec(wk.shape), _const_spec(wv.shape)],
        out_specs=[pl.BlockSpec((None, d // 2, m), lambda i: (i, 0, 0)),
                   pl.BlockSpec((None, m // 2, d), lambda i: (i, 0, 0))],
        out_shape=[jax.ShapeDtypeStruct((b, d // 2, m), jnp.uint32),
                   jax.ShapeDtypeStruct((b, m // 2, d), jnp.uint32)],
        compiler_params=pltpu.CompilerParams(dimension_semantics=("arbitrary",),
                                             vmem_limit_bytes=VMEM_LIMIT),
        name="memkv",
    )(mem, gain, wk, wv)


def _fill_halo_buf(buf, cur_ref, prev_ref, next_ref, has_prev, has_next):
    tc = cur_ref.shape[0]
    ncol = cur_ref.shape[1] // LANES
    for c in range(ncol):
        cs = pl.ds(c * LANES, LANES)
        buf[c, pl.ds(0, HALO), :] = jnp.where(has_prev, prev_ref[:, cs], 0.0)
        buf[c, pl.ds(HALO, tc), :] = cur_ref[:, cs]
        buf[c, pl.ds(HALO + tc, HALO), :] = jnp.where(has_next, next_ref[:, cs], 0.0)
        buf[c, pl.ds(2 * HALO + tc, PAD_ROWS - 2 * HALO), :] = jnp.zeros((PAD_ROWS - 2 * HALO, LANES), F32)


def _unpermute(qh, col, tc):
    sub = tc // SUBLANES
    return jnp.concatenate(
        [qh[col, pl.ds(j, sub, stride=SUBLANES), :] for j in range(SUBLANES)], axis=0)


def _window_sums(pbuf, t1, t2, group, tc):
    outs = []
    for c in (2 * group, 2 * group + 1):
        if group == 0:
            outs.append(_shifted(pbuf, c, 7, tc) + pbuf[c, pl.ds(8, tc), :])
        elif group == 1:
            outs.append((_shifted(pbuf, c, 6, tc) + _shifted(pbuf, c, 7, tc))
                        + (pbuf[c, pl.ds(8, tc), :] + _shifted(pbuf, c, 9, tc)))
        else:
            n2 = tc + 24 if group == 3 else tc + 16
            t1[0, pl.ds(0, n2), :] = pbuf[c, pl.ds(0, n2), :] + _shifted(pbuf, c, 1, n2)
            n4 = n2 - 8
            t2[0, pl.ds(0, n4), :] = t1[0, pl.ds(0, n4), :] + _shifted(t1, 0, 2, n4)
            if group == 2:
                outs.append(_shifted(t2, 0, 4, tc) + t2[0, pl.ds(8, tc), :])
            else:
                n8 = n4 - 8
                t1[0, pl.ds(0, n8), :] = t2[0, pl.ds(0, n8), :] + _shifted(t2, 0, 4, n8)
                outs.append(t1[0, pl.ds(0, tc), :] + t1[0, pl.ds(8, tc), :])
    return outs


def _pool_norm(i, nc, s_len, invbuf):
    ng, tc, lanes = invbuf.shape

    @pl.when((i == 0) | (i == nc - 1))
    def _():
        t = i * tc + lax.broadcasted_iota(jnp.int32, (tc, lanes), 0)
        for g in range(ng):
            half = POOL_WINDOWS[g] // 2
            count = jnp.minimum(t + half, s_len) - jnp.maximum(t - half, 0)
            invbuf[g] = 1.0 / count.astype(F32)

    if nc > 2:
        @pl.when(i == 1)
        def _():
            for g in range(ng):
                invbuf[g] = jnp.full((tc, lanes), 1.0 / POOL_WINDOWS[g], F32)


def _mixmlp_body(s_len, nc, x_ref, up_ref, upp_ref, upn_ref, af_ref, bf_ref, hb_ref, q_ref, gl_ref,
                 kt_ref, v_ref, poolw_ref, pscale_ref, wout_ref, gpost_ref, mpre_ref, mpost_ref,
                 w1_ref, w2_ref, o_ref,
                 pbuf, qh, carry_ref, t1, t2, mbuf, xmid, hbuf, a2buf, yacc, invbuf):
    s = pl.program_id(0)
    last = pl.num_programs(0) - 3
    i = jnp.minimum(s, last) % nc
    tc, d = x_ref.shape
    ncol = d // LANES
    n_ff = w1_ref.shape[1] // d

    @pl.when(s == 0)
    def _():
        xmid[...] = jnp.zeros(xmid.shape, F32)
        hbuf[...] = jnp.zeros(hbuf.shape, BF16)
        yacc[...] = jnp.zeros(yacc.shape, F32)

    _reset_carry(i == 0, carry_ref)
    _pool_norm(i, nc, s_len, invbuf)

    pw = 2 * LANES
    n_piece = d // pw

    def up_piece(f, n):
        def run():
            a = jnp.maximum(jnp.dot(hbuf[...], _wb(w1_ref[:, f * d + n * pw:f * d + (n + 1) * pw]),
                                    preferred_element_type=F32), 0.0)
            a2buf[f % 2, :, n * pw:(n + 1) * pw] = (a * a).astype(BF16)
        return run

    def down_piece(f, n):
        def run():
            part = jnp.dot(a2buf[f % 2], _wb(w2_ref[f * d // 2:(f + 1) * d // 2, n * pw:(n + 1) * pw]),
                           preferred_element_type=F32)
            if f == 0:
                yacc[:, n * pw:(n + 1) * pw] = part
            else:
                yacc[:, n * pw:(n + 1) * pw] += part
        return run

    queue = [("up", up_piece(0, n)) for n in range(n_piece)]
    for f in range(n_ff):
        for n in range(n_piece):
            if f + 1 < n_ff:
                queue.append(("up", up_piece(f + 1, n)))
            queue.append(("down", down_piece(f, n)))
    queue.reverse()

    def mlp_step(k=1):
        for _ in range(k):
            if queue:
                queue.pop()[1]()

    mlp_step()
    o_ref[...] = xmid[s % 2] + _rms(yacc[...], mpost_ref[...])

    mlp_step()
    _fill_halo_buf(pbuf, up_ref, upp_ref, upn_ref, i > 0, i < nc - 1)

    def emit(rows, h):
        y = h + hb_ref[rows, :]
        for c in range(ncol):
            qh[c, rows, :] = y[:, c * LANES:(c + 1) * LANES]

    _scan(False, af_ref, bf_ref, carry_ref, emit, tick=mlp_step, tick_every=16)

    gw = d // N_POOL_GROUPS
    for g in range(N_POOL_GROUPS):
        cols = pl.ds(g * gw, gw)
        inv = invbuf[g]
        sums = _window_sums(pbuf, t1, t2, g, tc)
        pooled = jnp.concatenate(
            [sums[k] * inv - pbuf[2 * g + k, pl.ds(HALO, tc), :] for k in range(2)], axis=1)
        mlp_step()
        yp = jnp.dot(pooled.astype(BF16), _wb(poolw_ref[g]), preferred_element_type=F32) * pscale_ref[:, cols]
        mlp_step()

        sc = jnp.dot(q_ref[:, cols], _wb(kt_ref[pl.ds(g * gw // 2, gw // 2), :]),
                     preferred_element_type=F32)
        e = jnp.exp(sc - jnp.max(sc, axis=-1, keepdims=True))
        p = e * (1.0 / jnp.sum(e, axis=-1, keepdims=True))
        mlp_step()
        ym = jnp.dot(p.astype(BF16), _wb(v_ref[:, cols]), preferred_element_type=F32)
        mlp_step()

        yl = jnp.concatenate([_unpermute(qh, 2 * g, tc), _unpermute(qh, 2 * g + 1, tc)], axis=1)

        merged = (yp + yl + ym) + jnp.tanh(gl_ref[:, cols]) * yp \
            + jnp.tanh(gl_ref[:, pl.ds(d + g * gw, gw)]) * yl \
            + jnp.tanh(gl_ref[:, pl.ds(2 * d + g * gw, gw)]) * ym
        mbuf[:, cols] = merged.astype(BF16)
        mlp_step()

    rest = len(queue)
    out = jnp.dot(mbuf[...], _wb(wout_ref[...]), preferred_element_type=F32)
    mlp_step(rest // 3)
    xnew = x_ref[...] + _rms(out, gpost_ref[...])
    xmid[s % 2] = xnew
    mlp_step(rest // 3)
    assert all(kind == "down" for kind, _ in queue)
    hbuf[...] = _rms(xnew, mpre_ref[...]).astype(BF16)
    mlp_step(len(queue))


def _mixmlp(x, up, af, bf, hb, q, gl, kt, v, poolw, pscale, wout, gpost, mpre, mpost, w1, w2):
    b, s_len, d = x.shape
    tc = min(TIME_CHUNK, s_len)
    nc = s_len // tc
    total = b * nc
    ncol = d // LANES
    per = tc // HALO
    last_halo = s_len // HALO - 1
    assert w1.shape[1] // d >= N_POOL_GROUPS

    def mix(s):
        c = jnp.minimum(s, total - 1)
        return c // nc, c % nc

    def cur(w):
        return pl.BlockSpec((None, tc, w), lambda s: (*mix(s), 0))

    prev = pl.BlockSpec((None, HALO, d), lambda s: (mix(s)[0], jnp.maximum(mix(s)[1] * per - 1, 0), 0))
    nxt = pl.BlockSpec((None, HALO, d), lambda s: (mix(s)[0], jnp.minimum((mix(s)[1] + 1) * per, last_halo), 0))
    per_batch = lambda a: pl.BlockSpec((None,) + a.shape[1:], lambda s: (mix(s)[0], 0, 0))

    def out_map(s):
        c = jnp.maximum(s - 2, 0)
        return c // nc, c % nc, 0

    scratch = [pltpu.VMEM((ncol, tc + PAD_ROWS, LANES), F32), pltpu.VMEM((ncol, tc, LANES), F32),
               pltpu.VMEM((SUBLANES, d), F32),
               pltpu.VMEM((1, tc + PAD_ROWS, LANES), F32), pltpu.VMEM((1, tc + PAD_ROWS, LANES), F32),
               pltpu.VMEM((tc, d), BF16), pltpu.VMEM((2, tc, d), F32),
               pltpu.VMEM((tc, d), BF16), pltpu.VMEM((2, tc, d), BF16), pltpu.VMEM((tc, d), F32),
               pltpu.VMEM((N_POOL_GROUPS, tc, LANES), F32)]
    return pl.pallas_call(
        functools.partial(_mixmlp_body, s_len, nc),
        grid=(total + 2,),
        in_specs=[cur(d), cur(d), prev, nxt, cur(d), cur(d), cur(d), cur(d), cur(3 * d),
                  per_batch(kt), per_batch(v),
                  _resident_spec(poolw.shape), _const_spec(pscale.shape), _resident_spec(wout.shape),
                  _const_spec(gpost.shape), _const_spec(mpre.shape), _const_spec(mpost.shape),
                  _resident_spec(w1.shape), _resident_spec(w2.shape)],
        out_specs=pl.BlockSpec((None, tc, d), out_map),
        out_shape=jax.ShapeDtypeStruct((b, s_len, d), F32),
        scratch_shapes=scratch,
        compiler_params=pltpu.CompilerParams(dimension_semantics=("arbitrary",),
                                             vmem_limit_bytes=VMEM_LIMIT),
        name="mixmlp",
    )(x, up, up, up, af, bf, hb, q, gl, kt, v, poolw, pscale, wout, gpost, mpre, mpost, w1, w2)


def _pack_body(w_ref, s_ref, o_ref):
    o_ref[...] = pltpu.bitcast((w_ref[...] * s_ref[...]).astype(BF16), jnp.uint32)


def _pack_rows(w, col_scale=None):
    nl, k, n = w.shape
    bn = min(n, 4 * LANES)
    scale = jnp.ones((1, n), F32) if col_scale is None else col_scale.reshape(1, n)
    return pl.pallas_call(
        _pack_body,
        grid=(nl, n // bn),
        in_specs=[pl.BlockSpec((None, k, bn), lambda l, j: (l, 0, j)), pl.BlockSpec((1, bn), lambda l, j: (0, j))],
        out_specs=pl.BlockSpec((None, k // 2, bn), lambda l, j: (l, 0, j)),
        out_shape=jax.ShapeDtypeStruct((nl, k // 2, n), jnp.uint32),
        compiler_params=pltpu.CompilerParams(dimension_semantics=("arbitrary", "arbitrary"),
                                             vmem_limit_bytes=VMEM_LIMIT),
        name="packw",
    )(w, scale)


def _wb(packed):
    return pltpu.bitcast(packed, BF16)


def _prep_layers(p):
    nl, d = p["w_in"].shape[:2]
    row = lambda a: a.reshape(1, -1)
    half = lambda n: jnp.full((n,), 0.5, F32)
    w_in = _pack_rows(p["w_in"], jnp.concatenate([jnp.ones((3 * d,), F32), half(3 * d)]))
    wg = jnp.concatenate([p["lru_wa"][:, 0], p["lru_wx"][:, 0], p["lru_wa"][:, 1], p["lru_wx"][:, 1]], axis=-1)
    nb, bk, bn = wg.shape[1:]
    wg = _pack_rows(wg.reshape(nl, nb * bk, bn), half(bn)).reshape(nl, nb, bk // 2, bn)
    ng, gk, gn = p["pool_w"].shape[1:]
    pool_w = _pack_rows(p["pool_w"].reshape(nl, ng * gk, gn)).reshape(nl, ng, gk // 2, gn)
    w_out = _pack_rows(p["w_out"], half(d))
    w1 = _pack_rows(p["mlp_w1"])
    w2 = _pack_rows(p["mlp_w2"])
    layers = []
    for l in range(nl):
        bias = 0.5 * jnp.stack([p["lru_ba"][l, 0], p["lru_bx"][l, 0], p["lru_ba"][l, 1], p["lru_bx"][l, 1]])
        layers.append(dict(
            g_pre=row(p["norm_mix_pre"][l]), g_post=row(p["norm_mix_post"][l]), g_mem=row(p["norm_mem"][l]),
            w_in=w_in[l], pool_w=pool_w[l], pool_scale=row(p["pool_scale"][l]),
            conv_w=p["conv_w"][l], conv_b=row(p["conv_b"][l]), wg=wg[l], bias=bias, lam=p["lru_lambda"][l],
            wk=p["w_kv"][l][:, :d].astype(BF16), wv=p["w_kv"][l][:, d:].astype(BF16),
            w_out=w_out[l],
            m_pre=row(p["norm_mlp_pre"][l]), m_post=row(p["norm_mlp_post"][l]),
            w1=w1[l], w2=w2[l],
        ))
    return layers


def _trunk(x, mem, layers):
    b, s_len, d = x.shape
    for lp in layers:
        up, q, gl, af, bf, hb = _inproj(x, lp["g_pre"], lp["w_in"], lp["conv_w"], lp["conv_b"],
                                        lp["wg"], lp["bias"], lp["lam"])
        kt, v = _memkv(mem, lp["g_mem"], lp["wk"], lp["wv"])
        x = _mixmlp(x, up, af, bf, hb, q, gl, kt, v, lp["pool_w"], lp["pool_scale"], lp["w_out"], lp["g_post"],
                    lp["m_pre"], lp["m_post"], lp["w1"], lp["w2"])
    return x


def kernel(x_prompt, x_sample, mem_prompt, mem_sample, norm_mix_pre, norm_mix_post, norm_mem, w_in, pool_w, pool_scale, conv_w, conv_b, lru_wa, lru_ba, lru_wx, lru_bx, lru_lambda, w_kv, w_out, norm_mlp_pre, norm_mlp_post, mlp_w1, mlp_w2):
    p = dict(norm_mix_pre=norm_mix_pre, norm_mix_post=norm_mix_post, norm_mem=norm_mem, w_in=w_in,
             pool_w=pool_w, pool_scale=pool_scale, conv_w=conv_w, conv_b=conv_b, lru_wa=lru_wa,
             lru_ba=lru_ba, lru_wx=lru_wx, lru_bx=lru_bx, lru_lambda=lru_lambda, w_kv=w_kv, w_out=w_out,
             norm_mlp_pre=norm_mlp_pre, norm_mlp_post=norm_mlp_post, mlp_w1=mlp_w1, mlp_w2=mlp_w2)
    layers = _prep_layers(p)
    return (_trunk(x_prompt, mem_prompt, layers), _trunk(x_sample, mem_sample, layers))
```
